```python
import math
import jax
import jax.numpy as jnp
from jax import lax
import numpy as np

D_MODEL = 2048
BATCH = 1
SEQ = 16384
DEPTH = 2

GRID_W = 64
CTX_LEN = 256
HEAD_DIM = 128
A_Q_HEADS = 6
A_KV_HEADS = 2
B_Q_HEADS = 6
B_KV_HEADS = 2
Q_BLOCK = 128
WINDOW = 128
SSM_GROUP = 16
SSM_WIDTH = D_MODEL // 4
SSM_GROUPS = SSM_WIDTH // SSM_GROUP
SSM_STATE = 64
A_Q_WIDTH = A_Q_HEADS * HEAD_DIM
A_KV_WIDTH = A_KV_HEADS * HEAD_DIM
B_Q_WIDTH = B_Q_HEADS * HEAD_DIM
B_KV_WIDTH = B_KV_HEADS * HEAD_DIM
MIX_WIDTH = A_Q_WIDTH + B_Q_WIDTH + SSM_WIDTH
IN_SIZES = (A_Q_WIDTH, A_KV_WIDTH, A_KV_WIDTH, A_Q_WIDTH,
            B_Q_WIDTH, B_KV_WIDTH, B_KV_WIDTH, B_Q_WIDTH,
            SSM_WIDTH, SSM_WIDTH)
IN_WIDTH = sum(IN_SIZES)
ROPE_THETA = 10000.0
NORM_EPS = 1e-6
NEG_INF = -1e30

kernel_name = "hymba_style_flow_backbone"


def rms_norm(x, gain):
    xf = x.astype(jnp.float32)
    y = xf * lax.rsqrt(jnp.mean(xf * xf, axis=-1, keepdims=True) + NORM_EPS)
    return (y * gain.astype(jnp.float32)).astype(x.dtype)


def split_in(p):
    out, start = [], 0
    for size in IN_SIZES:
        out.append(p[..., start:start + size])
        start += size
    return out


def to_heads(t):
    return t.reshape(t.shape[:-1] + (t.shape[-1] // HEAD_DIM, HEAD_DIM))


def axial_positions(n_tokens):
    n_rows = n_tokens // GRID_W
    row = jnp.repeat(jnp.arange(n_rows), GRID_W)
    col = jnp.tile(jnp.arange(GRID_W), n_rows)
    return row, col


def rope_1d(x, pos):
    half = x.shape[-1] // 2
    freqs = ROPE_THETA ** (-jnp.arange(half, dtype=jnp.float32) / half)
    ang = pos.astype(jnp.float32)[:, None] * freqs[None, :]
    cos = jnp.cos(ang)[None, :, None, :].astype(x.dtype)
    sin = jnp.sin(ang)[None, :, None, :].astype(x.dtype)
    x1, x2 = x[..., :half], x[..., half:]
    return jnp.concatenate([x1 * cos - x2 * sin, x1 * sin + x2 * cos], axis=-1)


def rope_axial(x, row, col):
    half = x.shape[-1] // 2
    return jnp.concatenate([rope_1d(x[..., :half], row), rope_1d(x[..., half:], col)], axis=-1)


def dense_attention(q, k, v, sink):
    bsz, n, hq, dh = q.shape
    hkv = k.shape[2]
    qg = q.reshape(bsz, n, hkv, hq // hkv, dh)
    s = jnp.einsum('blhgd,bmhd->bhglm', qg, k).astype(jnp.float32)
    if sink is not None:
        sk = sink.astype(jnp.float32).reshape(hkv, hq // hkv)[None, :, :, None, None]
        s = jnp.concatenate([s, jnp.broadcast_to(sk, s.shape[:-1] + (1,))], axis=-1)
        p = jax.nn.softmax(s, axis=-1)[..., :n]
    else:
        p = jax.nn.softmax(s, axis=-1)
    out = jnp.einsum('bhglm,bmhd->blhgd', p.astype(v.dtype), v)
    return out.reshape(bsz, n, hq * dh)


def global_attention(q, k, v, k_ctx, v_ctx):
    bsz, seq, hq, dh = q.shape
    hkv = k.shape[2]
    k_all = jnp.concatenate([k_ctx, k], axis=1)
    v_all = jnp.concatenate([v_ctx, v], axis=1)
    nb = seq // Q_BLOCK
    q_blocks = jnp.moveaxis(q.reshape(bsz, nb, Q_BLOCK, hkv, hq // hkv, dh), 1, 0)

    def attend(qb):
        s = jnp.einsum('bqhgd,bkhd->bhgqk', qb, k_all).astype(jnp.float32)
        p = jax.nn.softmax(s, axis=-1).astype(v_all.dtype)
        return jnp.einsum('bhgqk,bkhd->bqhgd', p, v_all)

    out = lax.map(attend, q_blocks)
    return jnp.moveaxis(out, 0, 1).reshape(bsz, seq, hq * dh)


def window_attention(q, k, v, k_ctx, v_ctx, sink):
    bsz, seq, hq, dh = q.shape
    hkv = k.shape[2]
    g = hq // hkv
    wb = WINDOW
    nb = seq // wb
    pad = ((0, 0), (wb, wb), (0, 0), (0, 0))
    kb = jnp.pad(k, pad).reshape(bsz, nb + 2, wb, hkv, dh)
    vb = jnp.pad(v, pad).reshape(bsz, nb + 2, wb, hkv, dh)
    k_win = jnp.concatenate([kb[:, :-2], kb[:, 1:-1], kb[:, 2:]], axis=2)
    v_win = jnp.concatenate([vb[:, :-2], vb[:, 1:-1], vb[:, 2:]], axis=2)
    qb = q.reshape(bsz, nb, wb, hkv, g, dh)
    s_band = jnp.einsum('bnqhgd,bnkhd->bnhgqk', qb, k_win).astype(jnp.float32)
    qi = jnp.arange(wb)[:, None]
    kj = jnp.arange(3 * wb)[None, :]
    key_pos = jnp.arange(nb)[:, None, None] * wb - wb + kj[None]
    rel = kj - qi
    valid = (rel >= 0) & (rel <= 2 * WINDOW) & (key_pos >= 0) & (key_pos < seq)
    s_band = jnp.where(valid[None, :, None, None], s_band, NEG_INF)
    s_ctx = jnp.einsum('bnqhgd,bchd->bnhgqc', qb, k_ctx).astype(jnp.float32)
    sk = sink.astype(jnp.float32).reshape(hkv, g)[None, None, :, :, None, None]
    s_sink = jnp.broadcast_to(sk, s_band.shape[:-1] + (1,))
    p = jax.nn.softmax(jnp.concatenate([s_band, s_ctx, s_sink], axis=-1), axis=-1).astype(v.dtype)
    n_band = 3 * wb
    n_ctx = k_ctx.shape[1]
    out = (jnp.einsum('bnhgqk,bnkhd->bnqhgd', p[..., :n_band], v_win)
           + jnp.einsum('bnhgqc,bchd->bnqhgd', p[..., n_band:n_band + n_ctx], v_ctx))
    return out.reshape(bsz, seq, hq * dh)


def s5_discretize(lam_re, lam_im, log_step, b_re, b_im):
    lam = lax.complex(lam_re.astype(jnp.float32), lam_im.astype(jnp.float32))
    step = jnp.exp(log_step.astype(jnp.float32))[:, None]
    lam_bar = jnp.exp(lam * step)
    b = lax.complex(b_re.astype(jnp.float32), b_im.astype(jnp.float32))
    b_bar = ((lam_bar - 1.0) / lam)[..., None] * b
    return lam_bar, b_bar


def s5_scan(u, lam_bar, b_bar, h0, reverse):
    bu = jnp.einsum('gph,blgh->blgp', b_bar, u.astype(jnp.float32))
    if h0 is not None:
        first = -1 if reverse else 0
        bu = bu.at[:, first].add(lam_bar * h0)
    a = jnp.broadcast_to(lam_bar, bu.shape)

    def combine(e1, e2):
        a1, b1 = e1
        a2, b2 = e2
        return a1 * a2, a2 * b1 + b2

    _, states = lax.associative_scan(combine, (a, bu), reverse=reverse, axis=1)
    return states


def s5_readout(states, c_mat):
    return jnp.einsum('ghp,blgp->blgh', c_mat, states).real


def s5_output(y_groups, u, d_skip, w_glu, b_glu):
    y = y_groups.reshape(u.shape).astype(u.dtype) + d_skip * u
    y = jax.nn.gelu(y)
    return y * jax.nn.sigmoid(y @ w_glu + b_glu)


def s5_mixer(u, u_ctx, lam_re, lam_im, log_step, b_re, b_im, c_re, c_im, d_skip, w_glu, b_glu, update_ctx):
    bsz, seq, _ = u.shape
    ug = u.reshape(bsz, seq, SSM_GROUPS, SSM_GROUP)
    ug_c = u_ctx.reshape(bsz, u_ctx.shape[1], SSM_GROUPS, SSM_GROUP)
    ys, ys_c = [], []
    for direction in range(2):
        reverse = direction == 1
        lam_bar, b_bar = s5_discretize(lam_re[direction], lam_im[direction], log_step[direction],
                                       b_re[direction], b_im[direction])
        c_mat = lax.complex(c_re[direction].astype(jnp.float32), c_im[direction].astype(jnp.float32))
        st_c = s5_scan(ug_c, lam_bar, b_bar, None, reverse)
        h0 = st_c[:, 0] if reverse else st_c[:, -1]
        st = s5_scan(ug, lam_bar, b_bar, h0, reverse)
        ys.append(s5_readout(st, c_mat))
        if update_ctx:
            ys_c.append(s5_readout(st_c, c_mat))
    y = s5_output(ys[0] + ys[1], u, d_skip, w_glu, b_glu)
    y_c = s5_output(ys_c[0] + ys_c[1], u_ctx, d_skip, w_glu, b_glu) if update_ctx else None
    return y, y_c


def hybrid_layer(x, ctx, c, c_ctx, w_ada, b_ada, norm_gain, w_in, a_q_gain, a_k_gain, b_sink,
                 lam_re, lam_im, log_step, sb_re, sb_im, sc_re, sc_im, s_d, w_glu, b_glu, w_out,
                 row, col, update_ctx):
    shift, scale, gate = jnp.split((jax.nn.silu(c) @ w_ada + b_ada)[:, None, :], 3, axis=-1)
    shift_c, scale_c, gate_c = jnp.split(jax.nn.silu(c_ctx) @ w_ada + b_ada, 3, axis=-1)
    h = rms_norm(x, norm_gain) * (1.0 + scale) + shift
    hc = rms_norm(ctx, norm_gain) * (1.0 + scale_c) + shift_c
    aq, ak, av, ag, bq, bk, bv, bg, su, sg = split_in(h @ w_in)
    aq_c, ak_c, av_c, ag_c, bq_c, bk_c, bv_c, bg_c, su_c, sg_c = split_in(hc @ w_in)
    attn_scale = HEAD_DIM ** -0.5

    qa = rope_axial(rms_norm(to_heads(aq), a_q_gain), row, col) * attn_scale
    ka = rope_axial(rms_norm(to_heads(ak), a_k_gain), row, col)
    ka_c = rms_norm(to_heads(ak_c), a_k_gain)
    va, va_c = to_heads(av), to_heads(av_c)
    out_a = global_attention(qa, ka, to_heads(av), ka_c, va_c) * jax.nn.silu(ag)

    qb = rope_axial(to_heads(bq), row, col) * attn_scale
    kb = rope_axial(to_heads(bk), row, col)
    kb_c, vb_c = to_heads(bk_c), to_heads(bv_c)
    out_b = window_attention(qb, kb, to_heads(bv), kb_c, vb_c, b_sink) * jax.nn.silu(bg)

    y_s, y_s_c = s5_mixer(su, su_c, lam_re, lam_im, log_step, sb_re, sb_im, sc_re, sc_im,
                          s_d, w_glu, b_glu, update_ctx)
    out_s = y_s * jax.nn.silu(sg)

    x = x + gate * (jnp.concatenate([out_a, out_b, out_s], axis=-1) @ w_out)

    if update_ctx:
        qa_c = rms_norm(to_heads(aq_c), a_q_gain) * attn_scale
        qb_c = to_heads(bq_c) * attn_scale
        out_a_c = dense_attention(qa_c, ka_c, va_c, None) * jax.nn.silu(ag_c)
        out_b_c = dense_attention(qb_c, kb_c, vb_c, b_sink) * jax.nn.silu(bg_c)
        out_s_c = y_s_c * jax.nn.silu(sg_c)
        ctx = ctx + gate_c * (jnp.concatenate([out_a_c, out_b_c, out_s_c], axis=-1) @ w_out)
    return x, ctx


def setup_inputs(seed: int = 0) -> dict:
    key = jax.random.key(seed)
    ks = jax.random.split(key, 24)
    f32 = jnp.float32
    nrm = lambda k, shape, s: jax.random.normal(k, shape, f32) * s
    lam_im_base = jnp.pi * jnp.arange(SSM_STATE, dtype=f32)
    return {
        "x": nrm(ks[0], (BATCH, SEQ, D_MODEL), 1.0),
        "c": nrm(ks[1], (BATCH, D_MODEL), 1.0),
        "ctx": nrm(ks[2], (BATCH, CTX_LEN, D_MODEL), 1.0),
        "c_ctx": nrm(ks[3], (D_MODEL,), 1.0),
        "w_ada": nrm(ks[4], (DEPTH, D_MODEL, 3 * D_MODEL), D_MODEL ** -0.5),
        "b_ada": nrm(ks[5], (DEPTH, 3 * D_MODEL), 0.02),
        "norm_gain": 1.0 + nrm(ks[6], (DEPTH, D_MODEL), 0.02),
        "w_in": nrm(ks[7], (DEPTH, D_MODEL, IN_WIDTH), D_MODEL ** -0.5),
        "a_q_gain": 1.0 + nrm(ks[8], (DEPTH, HEAD_DIM), 0.02),
        "a_k_gain": 1.0 + nrm(ks[9], (DEPTH, HEAD_DIM), 0.02),
        "b_sink": nrm(ks[10], (DEPTH, B_Q_HEADS), 0.5),
        "ssm_lambda_re": -0.5 + nrm(ks[11], (DEPTH, 2, SSM_GROUPS, SSM_STATE), 0.01),
        "ssm_lambda_im": lam_im_base + nrm(ks[12], (DEPTH, 2, SSM_GROUPS, SSM_STATE), 0.01),
        "ssm_log_step": jax.random.uniform(ks[13], (DEPTH, 2, SSM_GROUPS), f32,
                                           minval=math.log(1e-3), maxval=math.log(1e-1)),
        "ssm_b_re": nrm(ks[14], (DEPTH, 2, SSM_GROUPS, SSM_STATE, SSM_GROUP), (2 * SSM_GROUP) ** -0.5),
        "ssm_b_im": nrm(ks[15], (DEPTH, 2, SSM_GROUPS, SSM_STATE, SSM_GROUP), (2 * SSM_GROUP) ** -0.5),
        "ssm_c_re": nrm(ks[16], (DEPTH, 2, SSM_GROUPS, SSM_GROUP, SSM_STATE), SSM_STATE ** -0.5),
        "ssm_c_im": nrm(ks[17], (DEPTH, 2, SSM_GROUPS, SSM_GROUP, SSM_STATE), SSM_STATE ** -0.5),
        "ssm_d": nrm(ks[18], (DEPTH, SSM_WIDTH), 1.0),
        "w_glu": nrm(ks[19], (DEPTH, SSM_WIDTH, SSM_WIDTH), SSM_WIDTH ** -0.5),
        "b_glu": nrm(ks[20], (DEPTH, SSM_WIDTH), 0.02),
        "w_out": nrm(ks[21], (DEPTH, MIX_WIDTH, D_MODEL), MIX_WIDTH ** -0.5),
        "final_gain": 1.0 + nrm(ks[22], (D_MODEL,), 0.02),
    }


def reference(x, c, ctx, c_ctx, w_ada, b_ada, norm_gain, w_in, a_q_gain, a_k_gain, b_sink,
              ssm_lambda_re, ssm_lambda_im, ssm_log_step, ssm_b_re, ssm_b_im, ssm_c_re, ssm_c_im,
              ssm_d, w_glu, b_glu, w_out, final_gain):
    row, col = axial_positions(x.shape[1])
    for layer in range(DEPTH):
        x, ctx = hybrid_layer(
            x, ctx, c, c_ctx, w_ada[layer], b_ada[layer], norm_gain[layer], w_in[layer],
            a_q_gain[layer], a_k_gain[layer], b_sink[layer],
            ssm_lambda_re[layer], ssm_lambda_im[layer], ssm_log_step[layer],
            ssm_b_re[layer], ssm_b_im[layer], ssm_c_re[layer], ssm_c_im[layer],
            ssm_d[layer], w_glu[layer], b_glu[layer], w_out[layer],
            row, col, layer < DEPTH - 1)
    return rms_norm(x, final_gain)
```

```python
import functools
import math

import jax
import jax.numpy as jnp
from jax import lax
from jax.experimental import pallas as pl
from jax.experimental.pallas import tpu as pltpu

F32 = jnp.float32
BF16 = jnp.bfloat16

HEAD_DIM = 128
GRID_W = 64
Q_PER_KV = 3
KV_HEADS = 2
Q_HEADS = Q_PER_KV * KV_HEADS
WINDOW = 128
SSM_GROUP = 16
SSM_STATE = 64
SSM_CHUNK = 16
SSM_SEGMENTS = 8
ROPE_THETA = 10000.0
NORM_EPS = 1e-6
NEG_INF = -1e30
VMEM_LIMIT_V7X = 56 * 1024 * 1024

SLOT_AQ, SLOT_BQ, SLOT_AK, SLOT_BK, SLOT_AV, SLOT_BV = 0, 6, 12, 14, 16, 18
SLOT_SU, SLOT_AG, SLOT_BG, SLOT_SG = 20, 24, 30, 36
N_SLOTS = 40
PANEL_SLOTS = 8


def _params(sem, vmem=VMEM_LIMIT_V7X):
    return pltpu.CompilerParams(dimension_semantics=sem, vmem_limit_bytes=vmem)


def _dot(a, b):
    return jnp.dot(a, b, preferred_element_type=F32)


def _dot_nt(a, b):
    return lax.dot_general(a, b, (((1,), (1,)), ((), ())), preferred_element_type=F32)


def _dot_exact(a, b):
    return jnp.dot(a, b, preferred_element_type=F32, precision=lax.Precision.HIGHEST)


def _silu(x):
    return x * jax.nn.sigmoid(x)


def _mod_kernel(ct_ref, w_ref, b_ref, o_ref):
    s = _silu(ct_ref[...])
    w = w_ref[...]
    b = b_ref[...]
    o_ref[0:1, :] = jnp.sum(s[:, 0:1] * w, axis=0, keepdims=True) + b
    o_ref[1:2, :] = jnp.sum(s[:, 1:2] * w, axis=0, keepdims=True) + b


def _modulation(c_t, w_ada, b_ada):
    depth, d, n3 = w_ada.shape
    tn = 512
    return pl.pallas_call(
        _mod_kernel,
        out_shape=jax.ShapeDtypeStruct((depth, 2, n3), F32),
        grid=(depth, n3 // tn),
        in_specs=[
            pl.BlockSpec((d, 2), lambda l, j: (0, 0)),
            pl.BlockSpec((None, d, tn), lambda l, j: (l, 0, j)),
            pl.BlockSpec((None, 1, tn), lambda l, j: (l, 0, j)),
        ],
        out_specs=pl.BlockSpec((None, 2, tn), lambda l, j: (l, 0, j)),
        compiler_params=_params(("arbitrary", "arbitrary")),
        name="adaln_modulation",
    )(c_t, w_ada, b_ada.reshape(depth, 1, n3))


def _slot_config(slot):
    scale = HEAD_DIM ** -0.5
    if slot < SLOT_BQ:
        return 0, True, scale, False
    if slot < SLOT_AK:
        return None, True, scale, False
    if slot < SLOT_BK:
        return 1, True, None, False
    if slot < SLOT_AV:
        return None, True, None, False
    if slot < SLOT_AG:
        return None, False, None, False
    return None, False, None, True


def _inproj_kernel(*refs, rope):
    if rope:
        x_ref, shift_ref, scale_ref, gain_ref, w_ref, qk_ref, cos_ref, sin_ref, o_ref = refs
    else:
        x_ref, shift_ref, scale_ref, gain_ref, w_ref, qk_ref, o_ref = refs
    j = pl.program_id(0)
    x = x_ref[...]
    y = x * lax.rsqrt(jnp.mean(x * x, axis=-1, keepdims=True) + NORM_EPS) * gain_ref[...]
    h = y * (1.0 + scale_ref[...]) + shift_ref[...]
    acc = _dot(h.astype(BF16), w_ref[...])
    lane = lax.broadcasted_iota(jnp.int32, (1, HEAD_DIM), 1)
    first_half = (lane & 63) < 32

    for panel in range(N_SLOTS // PANEL_SLOTS):
        @pl.when(j == panel)
        def _(panel=panel):
            for k in range(PANEL_SLOTS):
                gain_row, rotary, scale, act = _slot_config(panel * PANEL_SLOTS + k)
                t = acc[:, k * HEAD_DIM:(k + 1) * HEAD_DIM]
                if gain_row is not None:
                    t = (t * lax.rsqrt(jnp.mean(t * t, axis=-1, keepdims=True) + NORM_EPS)
                         * qk_ref[gain_row:gain_row + 1, :])
                if rotary and rope:
                    partner = jnp.where(first_half, pltpu.roll(t, 96, 1), pltpu.roll(t, 32, 1))
                    t = t * cos_ref[...] + partner * sin_ref[...]
                if scale is not None:
                    t = t * scale
                if act:
                    t = _silu(t)
                o_ref[:, k * HEAD_DIM:(k + 1) * HEAD_DIM] = t.astype(BF16)


def _inproj(x, shift, scale, gain, w, qk_gain, rope_tables, bm):
    n, d = x.shape
    width = w.shape[1]
    bn = PANEL_SLOTS * HEAD_DIM
    rope = rope_tables is not None
    row = lambda j, i: (i, 0)
    const = lambda j, i: (0, 0)
    in_specs = [
        pl.BlockSpec((bm, d), row),
        pl.BlockSpec((1, d), const),
        pl.BlockSpec((1, d), const),
        pl.BlockSpec((1, d), const),
        pl.BlockSpec((d, bn), lambda j, i: (0, j)),
        pl.BlockSpec((2, HEAD_DIM), const),
    ]
    args = [x, shift, scale, gain, w, qk_gain]
    if rope:
        in_specs += [pl.BlockSpec((bm, HEAD_DIM), row), pl.BlockSpec((bm, HEAD_DIM), row)]
        args += list(rope_tables)
    return pl.pallas_call(
        functools.partial(_inproj_kernel, rope=rope),
        out_shape=jax.ShapeDtypeStruct((n, width), BF16),
        grid=(width // bn, n // bm),
        in_specs=in_specs,
        out_specs=pl.BlockSpec((bm, bn), lambda j, i: (i, j)),
        compiler_params=_params(("arbitrary", "arbitrary")),
        name="inproj_rope" if rope else "inproj_ctx",
    )(*args)


def _stack_heads(ref, rows=None):
    sl = slice(None) if rows is None else rows
    return jnp.concatenate([ref[sl, g * HEAD_DIM:(g + 1) * HEAD_DIM] for g in range(Q_PER_KV)], axis=0)


def _attn_global_kernel(q_ref, k_ref, v_ref, kc_ref, vc_ref, g_ref, o_ref, q3_s, m_s, l_s, acc_s, *, bq, nk):
    ki = pl.program_id(2)

    @pl.when(ki == 0)
    def _():
        q3 = _stack_heads(q_ref)
        q3_s[...] = q3
        s = _dot_nt(q3, kc_ref[...])
        m = jnp.max(s, axis=-1, keepdims=True)
        p = jnp.exp(s - m)
        m_s[...] = m
        l_s[...] = jnp.sum(p, axis=-1, keepdims=True)
        acc_s[...] = _dot(p.astype(BF16), vc_ref[...])

    s = _dot_nt(q3_s[...], k_ref[...])
    m_prev = m_s[...]
    m_new = jnp.maximum(m_prev, jnp.max(s, axis=-1, keepdims=True))
    alpha = jnp.exp(m_prev - m_new)
    p = jnp.exp(s - m_new)
    l_s[...] = alpha * l_s[...] + jnp.sum(p, axis=-1, keepdims=True)
    acc_s[...] = alpha * acc_s[...] + _dot(p.astype(BF16), v_ref[...])
    m_s[...] = m_new

    @pl.when(ki == nk - 1)
    def _():
        out = acc_s[...] / l_s[...]
        for g in range(Q_PER_KV):
            gate = g_ref[:, g * HEAD_DIM:(g + 1) * HEAD_DIM].astype(F32)
            o_ref[:, g * HEAD_DIM:(g + 1) * HEAD_DIM] = (out[g * bq:(g + 1) * bq] * gate).astype(BF16)


def _attn_global(p_lat, p_ctx, bq, bk):
    n = p_lat.shape[0]
    n_ctx = p_ctx.shape[0]
    nk = n // bk
    qw = Q_PER_KV * HEAD_DIM
    return pl.pallas_call(
        functools.partial(_attn_global_kernel, bq=bq, nk=nk),
        out_shape=jax.ShapeDtypeStruct((n, Q_HEADS * HEAD_DIM), BF16),
        grid=(KV_HEADS, n // bq, nk),
        in_specs=[
            pl.BlockSpec((bq, qw), lambda h, i, k: (i, SLOT_AQ // Q_PER_KV + h)),
            pl.BlockSpec((bk, HEAD_DIM), lambda h, i, k: (k, SLOT_AK + h)),
            pl.BlockSpec((bk, HEAD_DIM), lambda h, i, k: (k, SLOT_AV + h)),
            pl.BlockSpec((n_ctx, HEAD_DIM), lambda h, i, k: (0, SLOT_AK + h)),
            pl.BlockSpec((n_ctx, HEAD_DIM), lambda h, i, k: (0, SLOT_AV + h)),
            pl.BlockSpec((bq, qw), lambda h, i, k: (i, SLOT_AG // Q_PER_KV + h)),
        ],
        out_specs=pl.BlockSpec((bq, qw), lambda h, i, k: (i, h)),
        scratch_shapes=[
            pltpu.VMEM((Q_PER_KV * bq, HEAD_DIM), BF16),
            pltpu.VMEM((Q_PER_KV * bq, 1), F32),
            pltpu.VMEM((Q_PER_KV * bq, 1), F32),
            pltpu.VMEM((Q_PER_KV * bq, HEAD_DIM), F32),
        ],
        compiler_params=_params(("arbitrary", "arbitrary", "arbitrary")),
        name="attn_global",
    )(p_lat, p_lat, p_lat, p_ctx, p_ctx, p_lat)


def _sink_column(sink_ref, base, rows_per_head, t=None):
    row = lax.broadcasted_iota(jnp.int32, (Q_PER_KV * rows_per_head, 1), 0)
    get = (lambda g: sink_ref[base + g]) if t is None else (lambda g: sink_ref[t, base + g])
    return jnp.where(row < rows_per_head, get(0), jnp.where(row < 2 * rows_per_head, get(1), get(2)))


def _attn_window_kernel(sink_ref, q_ref, kp_ref, km_ref, kn_ref, vp_ref, vm_ref, vn_ref, kc_ref, vc_ref,
                        g_ref, o_ref, *, bq, n):
    h = pl.program_id(0)
    qi = pl.program_id(1)
    kcat = jnp.concatenate([kp_ref[...], km_ref[...], kn_ref[...]], axis=0)
    vcat = jnp.concatenate([vp_ref[...], vm_ref[...], vn_ref[...]], axis=0)
    kc = kc_ref[...]
    vc = vc_ref[...]
    span = 3 * WINDOW
    rows = Q_PER_KV * WINDOW
    r = lax.broadcasted_iota(jnp.int32, (rows, span), 0) & (WINDOW - 1)
    cidx = lax.broadcasted_iota(jnp.int32, (rows, span), 1)
    rel = cidx - r
    band = (rel >= 0) & (rel <= 2 * WINDOW)
    sink = _sink_column(sink_ref, h * Q_PER_KV, WINDOW)
    for sb in range(bq // WINDOW):
        q3 = _stack_heads(q_ref, slice(sb * WINDOW, (sb + 1) * WINDOW))
        kw = kcat[sb * WINDOW:sb * WINDOW + span]
        vw = vcat[sb * WINDOW:sb * WINDOW + span]
        key_pos = qi * bq + (sb - 1) * WINDOW + cidx
        valid = band & (key_pos >= 0) & (key_pos < n)
        s = jnp.where(valid, _dot_nt(q3, kw), NEG_INF)
        sc = _dot_nt(q3, kc)
        m = jnp.maximum(jnp.maximum(jnp.max(s, axis=-1, keepdims=True), jnp.max(sc, axis=-1, keepdims=True)), sink)
        p = jnp.exp(s - m)
        pc = jnp.exp(sc - m)
        den = jnp.sum(p, axis=-1, keepdims=True) + jnp.sum(pc, axis=-1, keepdims=True) + jnp.exp(sink - m)
        out = (_dot(p.astype(BF16), vw) + _dot(pc.astype(BF16), vc)) / den
        for g in range(Q_PER_KV):
            gate = g_ref[sb * WINDOW:(sb + 1) * WINDOW, g * HEAD_DIM:(g + 1) * HEAD_DIM].astype(F32)
            o_ref[sb * WINDOW:(sb + 1) * WINDOW, g * HEAD_DIM:(g + 1) * HEAD_DIM] = (
                out[g * WINDOW:(g + 1) * WINDOW] * gate).astype(BF16)


def _attn_window(p_lat, p_ctx, sink, bq):
    n = p_lat.shape[0]
    n_ctx = p_ctx.shape[0]
    qw = Q_PER_KV * HEAD_DIM
    per = bq // WINDOW
    last = n // WINDOW - 1
    prev_map = lambda slot: (lambda h, i: (jnp.maximum(i * per - 1, 0), slot + h))
    main_map = lambda slot: (lambda h, i: (i, slot + h))
    next_map = lambda slot: (lambda h, i: (jnp.minimum((i + 1) * per, last), slot + h))
    return pl.pallas_call(
        functools.partial(_attn_window_kernel, bq=bq, n=n),
        out_shape=jax.ShapeDtypeStruct((n, Q_HEADS * HEAD_DIM), BF16),
        grid=(KV_HEADS, n // bq),
        in_specs=[
            pl.BlockSpec(memory_space=pltpu.SMEM),
            pl.BlockSpec((bq, qw), lambda h, i: (i, SLOT_BQ // Q_PER_KV + h)),
            pl.BlockSpec((WINDOW, HEAD_DIM), prev_map(SLOT_BK)),
            pl.BlockSpec((bq, HEAD_DIM), main_map(SLOT_BK)),
            pl.BlockSpec((WINDOW, HEAD_DIM), next_map(SLOT_BK)),
            pl.BlockSpec((WINDOW, HEAD_DIM), prev_map(SLOT_BV)),
            pl.BlockSpec((bq, HEAD_DIM), main_map(SLOT_BV)),
            pl.BlockSpec((WINDOW, HEAD_DIM), next_map(SLOT_BV)),
            pl.BlockSpec((n_ctx, HEAD_DIM), lambda h, i: (0, SLOT_BK + h)),
            pl.BlockSpec((n_ctx, HEAD_DIM), lambda h, i: (0, SLOT_BV + h)),
            pl.BlockSpec((bq, qw), lambda h, i: (i, SLOT_BG // Q_PER_KV + h)),
        ],
        out_specs=pl.BlockSpec((bq, qw), lambda h, i: (i, h)),
        compiler_params=_params(("arbitrary", "arbitrary")),
        name="attn_window",
    )(sink, p_lat, p_lat, p_lat, p_lat, p_lat, p_lat, p_lat, p_ctx, p_ctx, p_lat)


def _attn_ctx_kernel(sink_ref, q_ref, k_ref, v_ref, g_ref, o_ref, *, n_ctx):
    idx = pl.program_id(0)
    t = idx // KV_HEADS
    h = idx % KV_HEADS
    q3 = _stack_heads(q_ref)
    s = _dot_nt(q3, k_ref[...])
    sink = _sink_column(sink_ref, h * Q_PER_KV, n_ctx, t=t)
    m = jnp.maximum(jnp.max(s, axis=-1, keepdims=True), sink)
    p = jnp.exp(s - m)
    den = jnp.sum(p, axis=-1, keepdims=True) + jnp.exp(sink - m)
    out = _dot(p.astype(BF16), v_ref[...]) / den
    for g in range(Q_PER_KV):
        gate = g_ref[:, g * HEAD_DIM:(g + 1) * HEAD_DIM].astype(F32)
        o_ref[:, g * HEAD_DIM:(g + 1) * HEAD_DIM] = (out[g * n_ctx:(g + 1) * n_ctx] * gate).astype(BF16)


def _attn_ctx(p_ctx, sinks):
    n_ctx = p_ctx.shape[0]
    qw = Q_PER_KV * HEAD_DIM
    return pl.pallas_call(
        functools.partial(_attn_ctx_kernel, n_ctx=n_ctx),
        out_shape=jax.ShapeDtypeStruct((n_ctx, 2 * Q_HEADS * HEAD_DIM), BF16),
        grid=(2 * KV_HEADS,),
        in_specs=[
            pl.BlockSpec(memory_space=pltpu.SMEM),
            pl.BlockSpec((n_ctx, qw), lambda i: (0, i)),
            pl.BlockSpec((n_ctx, HEAD_DIM), lambda i: (0, SLOT_AK + i)),
            pl.BlockSpec((n_ctx, HEAD_DIM), lambda i: (0, SLOT_AV + i)),
            pl.BlockSpec((n_ctx, qw), lambda i: (0, SLOT_AG // Q_PER_KV + i)),
        ],
        out_specs=pl.BlockSpec((n_ctx, qw), lambda i: (0, i)),
        compiler_params=_params(("arbitrary",)),
        name="attn_ctx",
    )(sinks, p_ctx, p_ctx, p_ctx, p_ctx)


def _complex_mul(a_re, a_im, b_re, b_im):
    return a_re * b_re - a_im * b_im, a_re * b_im + a_im * b_re


def _ssm_prep_kernel(lam_r_ref, lam_c_ref, bt_ref, ct_ref, m_ref, w_ref, v_ref, a_ref, *, chunks_per_segment):
    T = SSM_CHUNK
    lanes_pair = 2 * SSM_STATE
    width = T * SSM_GROUP
    lane_p = lax.broadcasted_iota(jnp.int32, (1, lanes_pair), 1)
    row_p = lax.broadcasted_iota(jnp.int32, (lanes_pair, 1), 0)
    lane_w = lax.broadcasted_iota(jnp.int32, (1, width), 1)
    lane_2w = lax.broadcasted_iota(jnp.int32, (1, 2 * width), 1)
    row_2w = lax.broadcasted_iota(jnp.int32, (2 * width, 1), 0)
    m_acc = [jnp.zeros((width, width), F32) for _ in range(2)]
    for d in range(2):
        lam_re_r, lam_im_r = lam_r_ref[d, 0:1, :], lam_r_ref[d, 1:2, :]
        step_r = jnp.exp(lam_r_ref[d, 2:3, :])
        lam_re_c, lam_im_c = lam_c_ref[d, :, 0:1], lam_c_ref[d, :, 1:2]
        step_c = jnp.exp(lam_c_ref[d, :, 2:3])

        def pow_row(e):
            mag = jnp.exp(lam_re_r * step_r * e)
            ang = lam_im_r * step_r * e
            return mag * jnp.cos(ang), mag * jnp.sin(ang)

        def pow_col(e):
            mag = jnp.exp(lam_re_c * step_c * e)
            ang = lam_im_c * step_c * e
            return mag * jnp.cos(ang), mag * jnp.sin(ang)

        lb_re, lb_im = pow_row(1.0)
        nr, ni = lb_re - 1.0, lb_im
        den = lam_re_r * lam_re_r + lam_im_r * lam_im_r
        q_re = (nr * lam_re_r + ni * lam_im_r) / den
        q_im = (ni * lam_re_r - nr * lam_im_r) / den
        bb_re, bb_im = _complex_mul(bt_ref[d, 0], bt_ref[d, 1], q_re, q_im)

        slot = lane_w >> 4
        lag = (slot if d == 0 else (T - 1) - slot).astype(F32)
        ek_re, ek_im = pow_col(lag)
        c_re, c_im = ct_ref[d, 0], ct_ref[d, 1]
        cl_re, cl_im = _complex_mul(c_re, c_im, ek_re, ek_im)
        for gg in range(2):
            own = (lane_p >> 6) == gg
            kt = (_dot_exact(jnp.where(own, bb_re, 0.0), cl_re)
                  - _dot_exact(jnp.where(own, bb_im, 0.0), cl_im))
            blocks = []
            for s in range(T):
                if d == 0:
                    blk = jnp.where(lane_w >= SSM_GROUP * s, pltpu.roll(kt, SSM_GROUP * s, 1), 0.0)
                else:
                    shift = (width - SSM_GROUP * (T - 1 - s)) % width
                    blk = jnp.where(lane_w < SSM_GROUP * (s + 1), pltpu.roll(kt, shift, 1), 0.0)
                blocks.append(blk)
            m_acc[gg] = m_acc[gg] + jnp.concatenate(blocks, axis=0)

        tok_r = (row_2w >> 4) & (T - 1)
        e_w = ((T - 1) - tok_r if d == 0 else tok_r).astype(F32)
        lw_re, lw_im = pow_row(e_w)
        bb512_re = jnp.concatenate([bb_re] * (2 * T), axis=0)
        bb512_im = jnp.concatenate([bb_im] * (2 * T), axis=0)
        w_re, w_im = _complex_mul(bb512_re, bb512_im, lw_re, lw_im)
        own_w = (row_2w >> 8) == (lane_p >> 6)
        w_ref[d, 0] = jnp.where(own_w, w_re, 0.0).astype(BF16)
        w_ref[d, 1] = jnp.where(own_w, w_im, 0.0).astype(BF16)

        tok_l = (lane_2w >> 4) & (T - 1)
        e_v = (tok_l + 1 if d == 0 else T - tok_l).astype(F32)
        ev_re, ev_im = pow_col(e_v)
        c2_re = jnp.concatenate([c_re, c_re], axis=1)
        c2_im = jnp.concatenate([c_im, c_im], axis=1)
        cv_re, cv_im = _complex_mul(c2_re, c2_im, ev_re, ev_im)
        own_v = (row_p >> 6) == (lane_2w >> 8)
        v_ref[d, 0] = jnp.where(own_v, cv_re, 0.0).astype(BF16)
        v_ref[d, 1] = jnp.where(own_v, -cv_im, 0.0).astype(BF16)

        a1_re, a1_im = pow_row(float(T))
        a2_re, a2_im = pow_row(float(T * chunks_per_segment))
        a_ref[d] = jnp.concatenate([a1_re, a1_im, a2_re, a2_im], axis=0)

    for gg in range(2):
        m_ref[gg] = m_acc[gg].astype(BF16)


def _ssm_prep(lam_re, lam_im, log_step, b_re, b_im, c_re, c_im, chunks_per_segment):
    n_dir, groups, state = lam_re.shape
    pairs = groups // 2
    lanes = 2 * state
    width = SSM_CHUNK * SSM_GROUP
    step = jnp.broadcast_to(log_step[:, :, None], lam_re.shape)
    rows = jnp.stack([lam_re, lam_im, step], axis=1)
    lam_r = rows.reshape(n_dir, 3, pairs, lanes).transpose(2, 0, 1, 3)
    lam_c = lam_r.transpose(0, 1, 3, 2)

    def b_layout(b):
        return b.reshape(n_dir, pairs, 2, state, SSM_GROUP).transpose(1, 0, 4, 2, 3).reshape(
            pairs, n_dir, SSM_GROUP, lanes)

    def c_layout(c):
        ct = c.reshape(n_dir, pairs, 2, SSM_GROUP, state).transpose(1, 0, 2, 4, 3).reshape(
            pairs, n_dir, lanes, SSM_GROUP)
        return jnp.tile(ct, (1, 1, 1, SSM_CHUNK))

    bt = jnp.stack([b_layout(b_re), b_layout(b_im)], axis=2)
    ct = jnp.stack([c_layout(c_re), c_layout(c_im)], axis=2)
    return pl.pallas_call(
        functools.partial(_ssm_prep_kernel, chunks_per_segment=chunks_per_segment),
        out_shape=(
            jax.ShapeDtypeStruct((groups, width, width), BF16),
            jax.ShapeDtypeStruct((pairs, n_dir, 2, 2 * width, lanes), BF16),
            jax.ShapeDtypeStruct((pairs, n_dir, 2, lanes, 2 * width), BF16),
            jax.ShapeDtypeStruct((pairs, n_dir, 4, lanes), F32),
        ),
        grid=(pairs,),
        in_specs=[
            pl.BlockSpec((None, n_dir, 3, lanes), lambda p: (p, 0, 0, 0)),
            pl.BlockSpec((None, n_dir, lanes, 3), lambda p: (p, 0, 0, 0)),
            pl.BlockSpec((None, n_dir, 2, SSM_GROUP, lanes), lambda p: (p, 0, 0, 0, 0)),
            pl.BlockSpec((None, n_dir, 2, lanes, width), lambda p: (p, 0, 0, 0, 0)),
        ],
        out_specs=(
            pl.BlockSpec((2, width, width), lambda p: (p, 0, 0)),
            pl.BlockSpec((None, n_dir, 2, 2 * width, lanes), lambda p: (p, 0, 0, 0, 0)),
            pl.BlockSpec((None, n_dir, 2, lanes, 2 * width), lambda p: (p, 0, 0, 0, 0)),
            pl.BlockSpec((None, n_dir, 4, lanes), lambda p: (p, 0, 0, 0)),
        ),
        compiler_params=_params(("arbitrary",)),
        name="ssm_prep",
    )(lam_r, lam_c, bt, ct)


def _ssm_core_kernel(u_ref, m_ref, w_ref, v_ref, a_ref, y_ref, hs, *, n_lat_chunks, n_ctx_chunks, pairs_per_step):
    seg = SSM_SEGMENTS
    cps = n_lat_chunks // seg
    lanes = 2 * SSM_STATE
    width = SSM_CHUNK * SSM_GROUP
    sub = lax.broadcasted_iota(jnp.int32, (seg, 1), 0)

    def step(a_re, a_im, h_re, h_im, s_re, s_im):
        return a_re * h_re - a_im * h_im + s_re, a_re * h_im + a_im * h_re + s_im

    for d in range(2):
        for pp in range(pairs_per_step):
            ucat = jnp.concatenate([u_ref[2 * pp], u_ref[2 * pp + 1]], axis=1)
            hs[d, 0, :, pp * lanes:(pp + 1) * lanes] = _dot(ucat, w_ref[pp, d, 0])
            hs[d, 1, :, pp * lanes:(pp + 1) * lanes] = _dot(ucat, w_ref[pp, d, 1])
        cat = lambda r: jnp.concatenate([a_ref[pp, d, r:r + 1, :] for pp in range(pairs_per_step)], axis=1)
        a_re, a_im, as_re, as_im = cat(0), cat(1), cat(2), cat(3)
        L = a_re.shape[1]

        h_re = jnp.zeros((1, L), F32)
        h_im = jnp.zeros((1, L), F32)
        order = range(n_ctx_chunks) if d == 0 else range(n_ctx_chunks - 1, -1, -1)
        for i in order:
            r = n_lat_chunks + i
            s_re, s_im = hs[d, 0, r:r + 1, :], hs[d, 1, r:r + 1, :]
            hs[d, 0, r:r + 1, :] = h_re
            hs[d, 1, r:r + 1, :] = h_im
            h_re, h_im = step(a_re, a_im, h_re, h_im, s_re, s_im)

        a8_re = jnp.broadcast_to(a_re, (seg, L))
        a8_im = jnp.broadcast_to(a_im, (seg, L))
        tile = lambda c: pl.ds(pl.multiple_of((c if d == 0 else cps - 1 - c) * seg, seg), seg)

        def local(c, carry):
            rows = tile(c)
            return step(a8_re, a8_im, carry[0], carry[1], hs[d, 0, rows, :], hs[d, 1, rows, :])

        e_re, e_im = lax.fori_loop(0, cps, local, (jnp.zeros((seg, L), F32), jnp.zeros((seg, L), F32)))

        hin_re = jnp.zeros((seg, L), F32)
        hin_im = jnp.zeros((seg, L), F32)
        ks = list(range(seg)) if d == 0 else list(range(seg - 1, -1, -1))
        for idx, k in enumerate(ks):
            if idx > 0:
                kp = ks[idx - 1]
                h_re, h_im = step(as_re, as_im, h_re, h_im, e_re[kp:kp + 1], e_im[kp:kp + 1])
            hin_re = jnp.where(sub == k, h_re, hin_re)
            hin_im = jnp.where(sub == k, h_im, hin_im)

        def final(c, carry):
            rows = tile(c)
            s_re, s_im = hs[d, 0, rows, :], hs[d, 1, rows, :]
            hs[d, 0, rows, :] = carry[0]
            hs[d, 1, rows, :] = carry[1]
            return step(a8_re, a8_im, carry[0], carry[1], s_re, s_im)

        lax.fori_loop(0, cps, final, (hin_re, hin_im))

    for gi in range(2 * pairs_per_step):
        pp, gg = gi // 2, gi % 2
        y = _dot(u_ref[gi], m_ref[gi])
        for d in range(2):
            cols = slice(gg * width, (gg + 1) * width)
            y = y + _dot(hs[d, 0, :, pp * lanes:(pp + 1) * lanes].astype(BF16), v_ref[pp, d, 0][:, cols])
            y = y + _dot(hs[d, 1, :, pp * lanes:(pp + 1) * lanes].astype(BF16), v_ref[pp, d, 1][:, cols])
        y_ref[gi] = y


def _ssm_core(u, m, w, v, a, n_lat_chunks, n_ctx_chunks):
    groups, rows, width = u.shape
    pairs_per_step = 4
    gps = 2 * pairs_per_step
    lanes = 2 * SSM_STATE
    n_dir = 2
    return pl.pallas_call(
        functools.partial(_ssm_core_kernel, n_lat_chunks=n_lat_chunks, n_ctx_chunks=n_ctx_chunks,
                          pairs_per_step=pairs_per_step),
        out_shape=jax.ShapeDtypeStruct((groups, rows, width), F32),
        grid=(groups // gps,),
        in_specs=[
            pl.BlockSpec((gps, rows, width), lambda q: (q, 0, 0)),
            pl.BlockSpec((gps, width, width), lambda q: (q, 0, 0)),
            pl.BlockSpec((pairs_per_step, n_dir, 2, 2 * width, lanes), lambda q: (q, 0, 0, 0, 0)),
            pl.BlockSpec((pairs_per_step, n_dir, 2, lanes, 2 * width), lambda q: (q, 0, 0, 0, 0)),
            pl.BlockSpec((pairs_per_step, n_dir, 4, lanes), lambda q: (q, 0, 0, 0)),
        ],
        out_specs=pl.BlockSpec((gps, rows, width), lambda q: (q, 0, 0)),
        scratch_shapes=[pltpu.VMEM((n_dir, 2, rows, pairs_per_step * lanes), F32)],
        compiler_params=_params(("arbitrary",)),
        name="ssm_core",
    )(u, m, w, v, a)


def _ssm_chunk_layout(u_lat, u_ctx):
    n = u_lat.shape[0]
    groups = u_lat.shape[1] // SSM_GROUP
    cps = n // SSM_CHUNK // SSM_SEGMENTS
    lat = u_lat.reshape(SSM_SEGMENTS, cps, SSM_CHUNK, groups, SSM_GROUP).transpose(3, 1, 0, 2, 4)
    lat = lat.reshape(groups, n // SSM_CHUNK, SSM_CHUNK * SSM_GROUP)
    ctx = u_ctx.reshape(-1, SSM_CHUNK, groups, SSM_GROUP).transpose(2, 0, 1, 3)
    ctx = ctx.reshape(groups, -1, SSM_CHUNK * SSM_GROUP)
    return jnp.concatenate([lat, ctx], axis=1)


def _ssm_token_layout(y, n):
    groups = y.shape[0]
    n_chunks = n // SSM_CHUNK
    cps = n_chunks // SSM_SEGMENTS
    lat = y[:, :n_chunks].reshape(groups, cps, SSM_SEGMENTS, SSM_CHUNK, SSM_GROUP).transpose(2, 1, 3, 0, 4)
    lat = lat.reshape(n, groups * SSM_GROUP)
    ctx = y[:, n_chunks:].reshape(groups, -1, SSM_CHUNK, SSM_GROUP).transpose(1, 2, 0, 3)
    ctx = ctx.reshape(-1, groups * SSM_GROUP)
    return lat, ctx


def _ssm_out_kernel(y_ref, u_ref, g_ref, d_ref, w_ref, b_ref, o_ref):
    y = y_ref[...] + d_ref[...] * u_ref[...].astype(F32)
    y = jax.nn.gelu(y, approximate=True)
    z = _dot(y.astype(BF16), w_ref[...]) + b_ref[...]
    o_ref[...] = (y * jax.nn.sigmoid(z) * g_ref[...].astype(F32)).astype(BF16)


def _ssm_out(y, p, d_skip, w_glu, b_glu, bm):
    n, width = y.shape
    blk = width // HEAD_DIM
    row = lambda i: (i, 0)
    const = lambda i: (0, 0)
    return pl.pallas_call(
        _ssm_out_kernel,
        out_shape=jax.ShapeDtypeStruct((n, width), BF16),
        grid=(n // bm,),
        in_specs=[
            pl.BlockSpec((bm, width), row),
            pl.BlockSpec((bm, width), lambda i: (i, SLOT_SU // blk)),
            pl.BlockSpec((bm, width), lambda i: (i, SLOT_SG // blk)),
            pl.BlockSpec((1, width), const),
            pl.BlockSpec((width, width), const),
            pl.BlockSpec((1, width), const),
        ],
        out_specs=pl.BlockSpec((bm, width), row),
        compiler_params=_params(("arbitrary",)),
        name="ssm_out",
    )(y, p, p, d_skip, w_glu, b_glu)


def _outproj_kernel(*refs, n_pieces, final):
    mix = refs[:n_pieces]
    ws = refs[n_pieces:2 * n_pieces]
    x_ref, gate_ref = refs[2 * n_pieces:2 * n_pieces + 2]
    rest = refs[2 * n_pieces + 2:]
    acc = _dot(mix[0][...], ws[0][...])
    for a, w in zip(mix[1:], ws[1:]):
        acc = acc + _dot(a[...], w[...])
    x = x_ref[...] + gate_ref[...] * acc
    if final:
        fg_ref, o_ref = rest
        x = x * lax.rsqrt(jnp.mean(x * x, axis=-1, keepdims=True) + NORM_EPS) * fg_ref[...]
    else:
        (o_ref,) = rest
    o_ref[...] = x


def _outproj(pieces, w_slabs, x, gate, final_gain, bm):
    n, d = x.shape
    row = lambda i: (i, 0)
    const = lambda i: (0, 0)
    in_specs = [pl.BlockSpec((bm, a.shape[1]), row) for a in pieces]
    in_specs += [pl.BlockSpec(w.shape, const) for w in w_slabs]
    in_specs += [pl.BlockSpec((bm, d), row), pl.BlockSpec((1, d), const)]
    args = list(pieces) + list(w_slabs) + [x, gate]
    final = final_gain is not None
    if final:
        in_specs.append(pl.BlockSpec((1, d), const))
        args.append(final_gain)
    return pl.pallas_call(
        functools.partial(_outproj_kernel, n_pieces=len(pieces), final=final),
        out_shape=jax.ShapeDtypeStruct((n, d), F32),
        grid=(n // bm,),
        in_specs=in_specs,
        out_specs=pl.BlockSpec((bm, d), row),
        compiler_params=_params(("arbitrary",)),
        name="outproj_final" if final else "outproj",
    )(*args)


def _rope_tables(n):
    pos = jnp.arange(n)
    half = HEAD_DIM // 4
    freqs = ROPE_THETA ** (-jnp.arange(half, dtype=F32) / half)
    ang_r = (pos // GRID_W).astype(F32)[:, None] * freqs[None, :]
    ang_c = (pos % GRID_W).astype(F32)[:, None] * freqs[None, :]
    cos = jnp.concatenate([jnp.cos(ang_r)] * 2 + [jnp.cos(ang_c)] * 2, axis=-1)
    sin = jnp.concatenate([-jnp.sin(ang_r), jnp.sin(ang_r), -jnp.sin(ang_c), jnp.sin(ang_c)], axis=-1)
    return cos, sin


def _reorder_w_in(w):
    hd = HEAD_DIM
    sizes = (6 * hd, 2 * hd, 2 * hd, 6 * hd, 6 * hd, 2 * hd, 2 * hd, 6 * hd, 4 * hd, 4 * hd)
    names = ("aq", "ak", "av", "ag", "bq", "bk", "bv", "bg", "su", "sg")
    parts, start = {}, 0
    for name, size in zip(names, sizes):
        parts[name] = w[:, start:start + size]
        start += size
    order = ("aq", "bq", "ak", "bk", "av", "bv", "su", "ag", "bg", "sg")
    return jnp.concatenate([parts[k] for k in order], axis=1).astype(BF16)


def kernel(x, c, ctx, c_ctx, w_ada, b_ada, norm_gain, w_in, a_q_gain, a_k_gain, b_sink, ssm_lambda_re,
           ssm_lambda_im, ssm_log_step, ssm_b_re, ssm_b_im, ssm_c_re, ssm_c_im, ssm_d, w_glu, b_glu, w_out,
           final_gain):
    depth, d = norm_gain.shape
    n = x.shape[1]
    n_ctx = ctx.shape[1]
    assert x.shape[0] == 1 and n % (SSM_CHUNK * SSM_SEGMENTS) == 0 and n % 512 == 0
    xs = x[0]
    cs = ctx[0]
    bm = 512
    attn_w = Q_HEADS * HEAD_DIM

    c_t = jnp.concatenate([c.reshape(d, 1), c_ctx.reshape(d, 1)], axis=1)
    mod = _modulation(c_t, w_ada, b_ada)
    tables = _rope_tables(n)
    n_lat_chunks = n // SSM_CHUNK
    n_ctx_chunks = n_ctx // SSM_CHUNK

    for layer in range(depth):
        last = layer == depth - 1
        shift, scale, gate = (mod[layer, :, i * d:(i + 1) * d] for i in range(3))
        w = _reorder_w_in(w_in[layer])
        gain = norm_gain[layer].reshape(1, d)
        qk_gain = jnp.stack([a_q_gain[layer], a_k_gain[layer]], axis=0)
        p_lat = _inproj(xs, shift[0:1], scale[0:1], gain, w, qk_gain, tables, bm)
        p_ctx = _inproj(cs, shift[1:2], scale[1:2], gain, w, qk_gain, None, n_ctx)

        mix_a = _attn_global(p_lat, p_ctx, bq=512, bk=512)
        mix_b = _attn_window(p_lat, p_ctx, b_sink[layer], bq=512)

        ops = _ssm_prep(ssm_lambda_re[layer], ssm_lambda_im[layer], ssm_log_step[layer], ssm_b_re[layer],
                        ssm_b_im[layer], ssm_c_re[layer], ssm_c_im[layer], n_lat_chunks // SSM_SEGMENTS)
        su = SLOT_SU * HEAD_DIM
        u = _ssm_chunk_layout(p_lat[:, su:su + 4 * HEAD_DIM], p_ctx[:, su:su + 4 * HEAD_DIM])
        y = _ssm_core(u, *ops, n_lat_chunks, n_ctx_chunks)
        y_lat, y_ctx = _ssm_token_layout(y, n)
        d_skip = ssm_d[layer].reshape(1, -1)
        wg = w_glu[layer].astype(BF16)
        bg = b_glu[layer].reshape(1, -1)
        mix_s = _ssm_out(y_lat, p_lat, d_skip, wg, bg, bm)

        wo = w_out[layer].astype(BF16)
        slabs = (wo[:attn_w], wo[attn_w:2 * attn_w], wo[2 * attn_w:])
        xs_new = _outproj((mix_a, mix_b, mix_s), slabs, xs, gate[0:1],
                          final_gain.reshape(1, d) if last else None, bm)
        if not last:
            sinks = jnp.stack([jnp.full_like(b_sink[layer], NEG_INF), b_sink[layer]], axis=0)
            mix_ab_c = _attn_ctx(p_ctx, sinks)
            mix_s_c = _ssm_out(y_ctx, p_ctx, d_skip, wg, bg, n_ctx)
            cs = _outproj((mix_ab_c, mix_s_c), (wo[:2 * attn_w], wo[2 * attn_w:]), cs, gate[1:2], None, n_ctx)
        xs = xs_new
    return xs[None]
```

```python
import functools
import math

import jax
import jax.numpy as jnp
from jax import lax
from jax.experimental import pallas as pl
from jax.experimental.pallas import tpu as pltpu

F32 = jnp.float32
BF16 = jnp.bfloat16

HEAD_DIM = 128
GRID_W = 64
Q_PER_KV = 3
KV_HEADS = 2
Q_HEADS = Q_PER_KV * KV_HEADS
WINDOW = 128
SSM_GROUP = 16
SSM_STATE = 64
SSM_CHUNK = 16
SSM_SEGMENTS = 8
ROPE_THETA = 10000.0
NORM_EPS = 1e-6
NEG_INF = -1e30
LOG2_E = math.log2(math.e)
VMEM_LIMIT_V7X = 56 * 1024 * 1024

SLOT_AQ, SLOT_BQ, SLOT_AK, SLOT_BK, SLOT_AV, SLOT_BV = 0, 6, 12, 14, 16, 18
SLOT_SU, SLOT_AG, SLOT_BG, SLOT_SG = 20, 24, 30, 36
N_SLOTS = 40
PANEL_SLOTS = 8


def _params(sem, vmem=VMEM_LIMIT_V7X):
    return pltpu.CompilerParams(dimension_semantics=sem, vmem_limit_bytes=vmem)


def _largest_block(total, limit):
    return max(b for b in range(HEAD_DIM, limit + 1, HEAD_DIM) if total % b == 0)


def _dot(a, b):
    return jnp.dot(a, b, preferred_element_type=F32)


def _dot_nt(a, b):
    return lax.dot_general(a, b, (((1,), (1,)), ((), ())), preferred_element_type=F32)


def _dot_exact(a, b):
    return jnp.dot(a, b, preferred_element_type=F32, precision=lax.Precision.HIGHEST)


def _silu(x):
    return x * jax.nn.sigmoid(x)


def _mod_kernel(ct_ref, w_ref, b_ref, o_ref):
    s = _silu(ct_ref[...])
    w = w_ref[...]
    b = b_ref[...]
    o_ref[0:1, :] = jnp.sum(s[:, 0:1] * w, axis=0, keepdims=True) + b
    o_ref[1:2, :] = jnp.sum(s[:, 1:2] * w, axis=0, keepdims=True) + b


def _modulation(c_t, w_ada, b_ada):
    depth, d, n3 = w_ada.shape
    tn = 512
    return pl.pallas_call(
        _mod_kernel,
        out_shape=jax.ShapeDtypeStruct((depth, 2, n3), F32),
        grid=(depth, n3 // tn),
        in_specs=[
            pl.BlockSpec((d, 2), lambda l, j: (0, 0)),
            pl.BlockSpec((None, d, tn), lambda l, j: (l, 0, j)),
            pl.BlockSpec((None, 1, tn), lambda l, j: (l, 0, j)),
        ],
        out_specs=pl.BlockSpec((None, 2, tn), lambda l, j: (l, 0, j)),
        compiler_params=_params(("arbitrary", "arbitrary")),
        name="adaln_modulation",
    )(c_t, w_ada, b_ada.reshape(depth, 1, n3))


def _slot_config(slot):
    scale = HEAD_DIM ** -0.5 * LOG2_E
    if slot < SLOT_BQ:
        return 0, True, scale, False
    if slot < SLOT_AK:
        return None, True, scale, False
    if slot < SLOT_BK:
        return 1, True, None, False
    if slot < SLOT_AV:
        return None, True, None, False
    if slot < SLOT_AG:
        return None, False, None, False
    return None, False, None, True


def _inproj_kernel(*refs, rope):
    if rope:
        x_ref, shift_ref, scale_ref, gain_ref, w_ref, qk_ref, cos_ref, sin_ref, o_ref = refs
    else:
        x_ref, shift_ref, scale_ref, gain_ref, w_ref, qk_ref, o_ref = refs
    j = pl.program_id(0)
    x = x_ref[...]
    y = x * lax.rsqrt(jnp.mean(x * x, axis=-1, keepdims=True) + NORM_EPS) * gain_ref[...]
    h = y * (1.0 + scale_ref[...]) + shift_ref[...]
    acc = _dot(h.astype(BF16), w_ref[...])
    lane = lax.broadcasted_iota(jnp.int32, (1, HEAD_DIM), 1)
    first_half = (lane & 63) < 32

    for panel in range(N_SLOTS // PANEL_SLOTS):
        @pl.when(j == panel)
        def _(panel=panel):
            for k in range(PANEL_SLOTS):
                gain_row, rotary, scale, act = _slot_config(panel * PANEL_SLOTS + k)
                t = acc[:, k * HEAD_DIM:(k + 1) * HEAD_DIM]
                if gain_row is not None:
                    t = (t * lax.rsqrt(jnp.mean(t * t, axis=-1, keepdims=True) + NORM_EPS)
                         * qk_ref[gain_row:gain_row + 1, :])
                if rotary and rope:
                    partner = jnp.where(first_half, pltpu.roll(t, 96, 1), pltpu.roll(t, 32, 1))
                    t = t * cos_ref[...] + partner * sin_ref[...]
                if scale is not None:
                    t = t * scale
                if act:
                    t = _silu(t)
                o_ref[:, k * HEAD_DIM:(k + 1) * HEAD_DIM] = t.astype(BF16)


def _inproj(x, shift, scale, gain, w, qk_gain, rope_tables, bm):
    n, d = x.shape
    width = w.shape[1]
    bn = PANEL_SLOTS * HEAD_DIM
    rope = rope_tables is not None
    row = lambda j, i: (i, 0)
    const = lambda j, i: (0, 0)
    in_specs = [
        pl.BlockSpec((bm, d), row),
        pl.BlockSpec((1, d), const),
        pl.BlockSpec((1, d), const),
        pl.BlockSpec((1, d), const),
        pl.BlockSpec((d, bn), lambda j, i: (0, j)),
        pl.BlockSpec((2, HEAD_DIM), const),
    ]
    args = [x, shift, scale, gain, w, qk_gain]
    if rope:
        in_specs += [pl.BlockSpec((bm, HEAD_DIM), row), pl.BlockSpec((bm, HEAD_DIM), row)]
        args += list(rope_tables)
    return pl.pallas_call(
        functools.partial(_inproj_kernel, rope=rope),
        out_shape=jax.ShapeDtypeStruct((n, width), BF16),
        grid=(width // bn, n // bm),
        in_specs=in_specs,
        out_specs=pl.BlockSpec((bm, bn), lambda j, i: (i, j)),
        compiler_params=_params(("arbitrary", "arbitrary")),
        name="inproj_rope" if rope else "inproj_ctx",
    )(*args)


def _stack_heads(ref, rows=None):
    sl = slice(None) if rows is None else rows
    return jnp.concatenate([ref[sl, g * HEAD_DIM:(g + 1) * HEAD_DIM] for g in range(Q_PER_KV)], axis=0)


def _attn_global_kernel(qt_ref, k0_ref, kn_ref, vt_ref, g_ref, o_ref, s_s, mb_s, m_s, l_s, acc_s, *, bq, nk, tq):
    ki = pl.program_id(2)
    tiles = [(g, c) for g in range(Q_PER_KV) for c in range(bq // tq)]

    def lanes(g, c):
        return slice(g * bq + c * tq, g * bq + (c + 1) * tq)

    def scores(k_ref, g, c):
        s = _dot(k_ref[...], qt_ref[g * HEAD_DIM:(g + 1) * HEAD_DIM, c * tq:(c + 1) * tq])
        s_s[:, lanes(g, c)] = s
        mb_s[:, lanes(g, c)] = jnp.max(s, axis=0, keepdims=True)

    @pl.when(ki == 0)
    def _():
        m_s[...] = jnp.full(m_s.shape, NEG_INF, F32)
        l_s[...] = jnp.zeros(l_s.shape, F32)
        acc_s[...] = jnp.zeros(acc_s.shape, F32)
        for g, c in tiles:
            scores(k0_ref, g, c)

    vt = vt_ref[...]
    for g, c in tiles:
        cols = lanes(g, c)
        m_prev = m_s[:, cols]
        m_new = jnp.maximum(m_prev, mb_s[:, cols])
        alpha = jnp.exp2(m_prev - m_new)
        p = jnp.exp2(s_s[:, cols] - m_new)
        l_s[:, cols] = alpha * l_s[:, cols] + jnp.sum(p, axis=0, keepdims=True)
        acc_s[:, cols] = alpha * acc_s[:, cols] + _dot(vt, p.astype(BF16))
        m_s[:, cols] = m_new
        scores(kn_ref, g, c)

    @pl.when(ki == nk - 1)
    def _():
        for g in range(Q_PER_KV):
            cols = slice(g * bq, (g + 1) * bq)
            out_t = acc_s[:, cols] / l_s[:, cols]
            gate = g_ref[:, g * HEAD_DIM:(g + 1) * HEAD_DIM].astype(F32)
            o_ref[:, g * HEAD_DIM:(g + 1) * HEAD_DIM] = (out_t.T * gate).astype(BF16)


def _attn_global(p_lat, p_ctx, bq, bk):
    n = p_lat.shape[0]
    hd = HEAD_DIM
    qw = Q_PER_KV * hd
    k_all = jnp.concatenate([p_ctx[:, SLOT_AK * hd:SLOT_BK * hd], p_lat[:, SLOT_AK * hd:SLOT_BK * hd]], axis=0)
    v_all = jnp.concatenate([p_ctx[:, SLOT_AV * hd:SLOT_BV * hd], p_lat[:, SLOT_AV * hd:SLOT_BV * hd]], axis=0)
    vt_all = v_all.T
    qt = p_lat[:, SLOT_AQ * hd:SLOT_BQ * hd].T
    n_kv = k_all.shape[0]
    assert n_kv % bk == 0
    nk = n_kv // bk
    width = Q_PER_KV * bq
    return pl.pallas_call(
        functools.partial(_attn_global_kernel, bq=bq, nk=nk, tq=512),
        out_shape=jax.ShapeDtypeStruct((n, Q_HEADS * hd), BF16),
        grid=(KV_HEADS, n // bq, nk),
        in_specs=[
            pl.BlockSpec((qw, bq), lambda h, i, k: (h, i)),
            pl.BlockSpec((bk, hd), lambda h, i, k: (0, h)),
            pl.BlockSpec((bk, hd), lambda h, i, k: (jnp.minimum(k + 1, nk - 1), h)),
            pl.BlockSpec((hd, bk), lambda h, i, k: (h, k)),
            pl.BlockSpec((bq, qw), lambda h, i, k: (i, SLOT_AG // Q_PER_KV + h)),
        ],
        out_specs=pl.BlockSpec((bq, qw), lambda h, i, k: (i, h)),
        scratch_shapes=[
            pltpu.VMEM((bk, width), F32),
            pltpu.VMEM((1, width), F32),
            pltpu.VMEM((1, width), F32),
            pltpu.VMEM((1, width), F32),
            pltpu.VMEM((hd, width), F32),
        ],
        compiler_params=_params(("arbitrary", "arbitrary", "arbitrary")),
        name="attn_global",
    )(qt, k_all, k_all, vt_all, p_lat)


def _sink_column(sink_ref, base, rows_per_head, t=None):
    row = lax.broadcasted_iota(jnp.int32, (Q_PER_KV * rows_per_head, 1), 0)
    get = (lambda g: sink_ref[base + g]) if t is None else (lambda g: sink_ref[t, base + g])
    get = functools.partial(lambda f, g: f(g) * LOG2_E, get)
    return jnp.where(row < rows_per_head, get(0), jnp.where(row < 2 * rows_per_head, get(1), get(2)))


def _attn_window_kernel(sink_ref, q_ref, kp_ref, km_ref, kn_ref, vp_ref, vm_ref, vn_ref, kc_ref, vc_ref,
                        g_ref, o_ref, *, bq, n):
    h = pl.program_id(0)
    qi = pl.program_id(1)
    kcat = jnp.concatenate([kp_ref[...], km_ref[...], kn_ref[...]], axis=0)
    vcat = jnp.concatenate([vp_ref[...], vm_ref[...], vn_ref[...]], axis=0)
    kc = kc_ref[...]
    vc = vc_ref[...]
    span = 3 * WINDOW
    rows = Q_PER_KV * WINDOW
    r = lax.broadcasted_iota(jnp.int32, (rows, span), 0) & (WINDOW - 1)
    cidx = lax.broadcasted_iota(jnp.int32, (rows, span), 1)
    rel = cidx - r
    band = (rel >= 0) & (rel <= 2 * WINDOW)
    sink = _sink_column(sink_ref, h * Q_PER_KV, WINDOW)
    for sb in range(bq // WINDOW):
        q3 = _stack_heads(q_ref, slice(sb * WINDOW, (sb + 1) * WINDOW))
        kw = kcat[sb * WINDOW:sb * WINDOW + span]
        vw = vcat[sb * WINDOW:sb * WINDOW + span]
        key_pos = qi * bq + (sb - 1) * WINDOW + cidx
        valid = band & (key_pos >= 0) & (key_pos < n)
        s = jnp.where(valid, _dot_nt(q3, kw), NEG_INF)
        sc = _dot_nt(q3, kc)
        m = jnp.maximum(jnp.maximum(jnp.max(s, axis=-1, keepdims=True), jnp.max(sc, axis=-1, keepdims=True)), sink)
        p = jnp.exp2(s - m)
        pc = jnp.exp2(sc - m)
        den = jnp.sum(p, axis=-1, keepdims=True) + jnp.sum(pc, axis=-1, keepdims=True) + jnp.exp2(sink - m)
        out = (_dot(p.astype(BF16), vw) + _dot(pc.astype(BF16), vc)) / den
        for g in range(Q_PER_KV):
            gate = g_ref[sb * WINDOW:(sb + 1) * WINDOW, g * HEAD_DIM:(g + 1) * HEAD_DIM].astype(F32)
            o_ref[sb * WINDOW:(sb + 1) * WINDOW, g * HEAD_DIM:(g + 1) * HEAD_DIM] = (
                out[g * WINDOW:(g + 1) * WINDOW] * gate).astype(BF16)


def _attn_window(p_lat, p_ctx, sink, bq):
    n = p_lat.shape[0]
    n_ctx = p_ctx.shape[0]
    qw = Q_PER_KV * HEAD_DIM
    per = bq // WINDOW
    last = n // WINDOW - 1
    prev_map = lambda slot: (lambda h, i: (jnp.maximum(i * per - 1, 0), slot + h))
    main_map = lambda slot: (lambda h, i: (i, slot + h))
    next_map = lambda slot: (lambda h, i: (jnp.minimum((i + 1) * per, last), slot + h))
    return pl.pallas_call(
        functools.partial(_attn_window_kernel, bq=bq, n=n),
        out_shape=jax.ShapeDtypeStruct((n, Q_HEADS * HEAD_DIM), BF16),
        grid=(KV_HEADS, n // bq),
        in_specs=[
            pl.BlockSpec(memory_space=pltpu.SMEM),
            pl.BlockSpec((bq, qw), lambda h, i: (i, SLOT_BQ // Q_PER_KV + h)),
            pl.BlockSpec((WINDOW, HEAD_DIM), prev_map(SLOT_BK)),
            pl.BlockSpec((bq, HEAD_DIM), main_map(SLOT_BK)),
            pl.BlockSpec((WINDOW, HEAD_DIM), next_map(SLOT_BK)),
            pl.BlockSpec((WINDOW, HEAD_DIM), prev_map(SLOT_BV)),
            pl.BlockSpec((bq, HEAD_DIM), main_map(SLOT_BV)),
            pl.BlockSpec((WINDOW, HEAD_DIM), next_map(SLOT_BV)),
            pl.BlockSpec((n_ctx, HEAD_DIM), lambda h, i: (0, SLOT_BK + h)),
            pl.BlockSpec((n_ctx, HEAD_DIM), lambda h, i: (0, SLOT_BV + h)),
            pl.BlockSpec((bq, qw), lambda h, i: (i, SLOT_BG // Q_PER_KV + h)),
        ],
        out_specs=pl.BlockSpec((bq, qw), lambda h, i: (i, h)),
        compiler_params=_params(("arbitrary", "arbitrary")),
        name="attn_window",
    )(sink, p_lat, p_lat, p_lat, p_lat, p_lat, p_lat, p_lat, p_ctx, p_ctx, p_lat)


def _attn_ctx_kernel(sink_ref, q_ref, k_ref, v_ref, g_ref, o_ref, *, n_ctx):
    idx = pl.program_id(0)
    t = idx // KV_HEADS
    h = idx % KV_HEADS
    q3 = _stack_heads(q_ref)
    s = _dot_nt(q3, k_ref[...])
    sink = _sink_column(sink_ref, h * Q_PER_KV, n_ctx, t=t)
    m = jnp.maximum(jnp.max(s, axis=-1, keepdims=True), sink)
    p = jnp.exp2(s - m)
    den = jnp.sum(p, axis=-1, keepdims=True) + jnp.exp2(sink - m)
    out = _dot(p.astype(BF16), v_ref[...]) / den
    for g in range(Q_PER_KV):
        gate = g_ref[:, g * HEAD_DIM:(g + 1) * HEAD_DIM].astype(F32)
        o_ref[:, g * HEAD_DIM:(g + 1) * HEAD_DIM] = (out[g * n_ctx:(g + 1) * n_ctx] * gate).astype(BF16)


def _attn_ctx(p_ctx, sinks):
    n_ctx = p_ctx.shape[0]
    qw = Q_PER_KV * HEAD_DIM
    return pl.pallas_call(
        functools.partial(_attn_ctx_kernel, n_ctx=n_ctx),
        out_shape=jax.ShapeDtypeStruct((n_ctx, 2 * Q_HEADS * HEAD_DIM), BF16),
        grid=(2 * KV_HEADS,),
        in_specs=[
            pl.BlockSpec(memory_space=pltpu.SMEM),
            pl.BlockSpec((n_ctx, qw), lambda i: (0, i)),
            pl.BlockSpec((n_ctx, HEAD_DIM), lambda i: (0, SLOT_AK + i)),
            pl.BlockSpec((n_ctx, HEAD_DIM), lambda i: (0, SLOT_AV + i)),
            pl.BlockSpec((n_ctx, qw), lambda i: (0, SLOT_AG // Q_PER_KV + i)),
        ],
        out_specs=pl.BlockSpec((n_ctx, qw), lambda i: (0, i)),
        compiler_params=_params(("arbitrary",)),
        name="attn_ctx",
    )(sinks, p_ctx, p_ctx, p_ctx, p_ctx)


def _complex_mul(a_re, a_im, b_re, b_im):
    return a_re * b_re - a_im * b_im, a_re * b_im + a_im * b_re


def _ssm_prep_kernel(lam_r_ref, lam_c_ref, bt_ref, ct_ref, m_ref, w_ref, v_ref, a_ref, *, chunks_per_segment):
    T = SSM_CHUNK
    lanes_pair = 2 * SSM_STATE
    width = T * SSM_GROUP
    lane_p = lax.broadcasted_iota(jnp.int32, (1, lanes_pair), 1)
    row_p = lax.broadcasted_iota(jnp.int32, (lanes_pair, 1), 0)
    lane_w = lax.broadcasted_iota(jnp.int32, (1, width), 1)
    lane_2w = lax.broadcasted_iota(jnp.int32, (1, 2 * width), 1)
    row_2w = lax.broadcasted_iota(jnp.int32, (2 * width, 1), 0)
    m_acc = [jnp.zeros((width, width), F32) for _ in range(2)]
    for d in range(2):
        lam_re_r, lam_im_r = lam_r_ref[d, 0:1, :], lam_r_ref[d, 1:2, :]
        step_r = jnp.exp(lam_r_ref[d, 2:3, :])
        lam_re_c, lam_im_c = lam_c_ref[d, :, 0:1], lam_c_ref[d, :, 1:2]
        step_c = jnp.exp(lam_c_ref[d, :, 2:3])

        def pow_row(e):
            mag = jnp.exp(lam_re_r * step_r * e)
            ang = lam_im_r * step_r * e
            return mag * jnp.cos(ang), mag * jnp.sin(ang)

        def pow_col(e):
            mag = jnp.exp(lam_re_c * step_c * e)
            ang = lam_im_c * step_c * e
            return mag * jnp.cos(ang), mag * jnp.sin(ang)

        lb_re, lb_im = pow_row(1.0)
        nr, ni = lb_re - 1.0, lb_im
        den = lam_re_r * lam_re_r + lam_im_r * lam_im_r
        q_re = (nr * lam_re_r + ni * lam_im_r) / den
        q_im = (ni * lam_re_r - nr * lam_im_r) / den
        bb_re, bb_im = _complex_mul(bt_ref[d, 0], bt_ref[d, 1], q_re, q_im)

        slot = lane_w >> 4
        lag = (slot if d == 0 else (T - 1) - slot).astype(F32)
        ek_re, ek_im = pow_col(lag)
        c_re, c_im = ct_ref[d, 0], ct_ref[d, 1]
        cl_re, cl_im = _complex_mul(c_re, c_im, ek_re, ek_im)
        for gg in range(2):
            own = (lane_p >> 6) == gg
            kt = (_dot_exact(jnp.where(own, bb_re, 0.0), cl_re)
                  - _dot_exact(jnp.where(own, bb_im, 0.0), cl_im))
            blocks = []
            for s in range(T):
                if d == 0:
                    blk = jnp.where(lane_w >= SSM_GROUP * s, pltpu.roll(kt, SSM_GROUP * s, 1), 0.0)
                else:
                    shift = (width - SSM_GROUP * (T - 1 - s)) % width
                    blk = jnp.where(lane_w < SSM_GROUP * (s + 1), pltpu.roll(kt, shift, 1), 0.0)
                blocks.append(blk)
            m_acc[gg] = m_acc[gg] + jnp.concatenate(blocks, axis=0)

        tok_r = (row_2w >> 4) & (T - 1)
        e_w = ((T - 1) - tok_r if d == 0 else tok_r).astype(F32)
        lw_re, lw_im = pow_row(e_w)
        bb512_re = jnp.concatenate([bb_re] * (2 * T), axis=0)
        bb512_im = jnp.concatenate([bb_im] * (2 * T), axis=0)
        w_re, w_im = _complex_mul(bb512_re, bb512_im, lw_re, lw_im)
        own_w = (row_2w >> 8) == (lane_p >> 6)
        w_ref[d, 0] = jnp.where(own_w, w_re, 0.0).astype(BF16)
        w_ref[d, 1] = jnp.where(own_w, w_im, 0.0).astype(BF16)

        tok_l = (lane_2w >> 4) & (T - 1)
        e_v = (tok_l + 1 if d == 0 else T - tok_l).astype(F32)
        ev_re, ev_im = pow_col(e_v)
        c2_re = jnp.concatenate([c_re, c_re], axis=1)
        c2_im = jnp.concatenate([c_im, c_im], axis=1)
        cv_re, cv_im = _complex_mul(c2_re, c2_im, ev_re, ev_im)
        own_v = (row_p >> 6) == (lane_2w >> 8)
        v_ref[d, 0] = jnp.where(own_v, cv_re, 0.0).astype(BF16)
        v_ref[d, 1] = jnp.where(own_v, -cv_im, 0.0).astype(BF16)

        a1_re, a1_im = pow_row(float(T))
        a2_re, a2_im = pow_row(float(T * chunks_per_segment))
        a_ref[d] = jnp.concatenate([a1_re, a1_im, a2_re, a2_im], axis=0)

    for gg in range(2):
        m_ref[gg] = m_acc[gg].astype(BF16)


def _ssm_prep(lam_re, lam_im, log_step, b_re, b_im, c_re, c_im, chunks_per_segment):
    n_dir, groups, state = lam_re.shape
    pairs = groups // 2
    lanes = 2 * state
    width = SSM_CHUNK * SSM_GROUP
    step = jnp.broadcast_to(log_step[:, :, None], lam_re.shape)
    rows = jnp.stack([lam_re, lam_im, step], axis=1)
    lam_r = rows.reshape(n_dir, 3, pairs, lanes).transpose(2, 0, 1, 3)
    lam_c = lam_r.transpose(0, 1, 3, 2)

    def b_layout(b):
        return b.reshape(n_dir, pairs, 2, state, SSM_GROUP).transpose(1, 0, 4, 2, 3).reshape(
            pairs, n_dir, SSM_GROUP, lanes)

    def c_layout(c):
        ct = c.reshape(n_dir, pairs, 2, SSM_GROUP, state).transpose(1, 0, 2, 4, 3).reshape(
            pairs, n_dir, lanes, SSM_GROUP)
        return jnp.tile(ct, (1, 1, 1, SSM_CHUNK))

    bt = jnp.stack([b_layout(b_re), b_layout(b_im)], axis=2)
    ct = jnp.stack([c_layout(c_re), c_layout(c_im)], axis=2)
    return pl.pallas_call(
        functools.partial(_ssm_prep_kernel, chunks_per_segment=chunks_per_segment),
        out_shape=(
            jax.ShapeDtypeStruct((groups, width, width), BF16),
            jax.ShapeDtypeStruct((pairs, n_dir, 2, 2 * width, lanes), BF16),
            jax.ShapeDtypeStruct((pairs, n_dir, 2, lanes, 2 * width), BF16),
            jax.ShapeDtypeStruct((pairs, n_dir, 4, lanes), F32),
        ),
        grid=(pairs,),
        in_specs=[
            pl.BlockSpec((None, n_dir, 3, lanes), lambda p: (p, 0, 0, 0)),
            pl.BlockSpec((None, n_dir, lanes, 3), lambda p: (p, 0, 0, 0)),
            pl.BlockSpec((None, n_dir, 2, SSM_GROUP, lanes), lambda p: (p, 0, 0, 0, 0)),
            pl.BlockSpec((None, n_dir, 2, lanes, width), lambda p: (p, 0, 0, 0, 0)),
        ],
        out_specs=(
            pl.BlockSpec((2, width, width), lambda p: (p, 0, 0)),
            pl.BlockSpec((None, n_dir, 2, 2 * width, lanes), lambda p: (p, 0, 0, 0, 0)),
            pl.BlockSpec((None, n_dir, 2, lanes, 2 * width), lambda p: (p, 0, 0, 0, 0)),
            pl.BlockSpec((None, n_dir, 4, lanes), lambda p: (p, 0, 0, 0)),
        ),
        compiler_params=_params(("arbitrary",)),
        name="ssm_prep",
    )(lam_r, lam_c, bt, ct)


def _ssm_core_kernel(u_ref, m_ref, w_ref, v_ref, a_ref, y_ref, hs, *, n_lat_chunks, n_ctx_chunks, pairs_per_step):
    seg = SSM_SEGMENTS
    cps = n_lat_chunks // seg
    lanes = 2 * SSM_STATE
    width = SSM_CHUNK * SSM_GROUP
    sub = lax.broadcasted_iota(jnp.int32, (seg, 1), 0)

    def step(a_re, a_im, h_re, h_im, s_re, s_im):
        return a_re * h_re - a_im * h_im + s_re, a_re * h_im + a_im * h_re + s_im

    for d in range(2):
        for pp in range(pairs_per_step):
            ucat = jnp.concatenate([u_ref[2 * pp], u_ref[2 * pp + 1]], axis=1)
            hs[d, 0, :, pp * lanes:(pp + 1) * lanes] = _dot(ucat, w_ref[pp, d, 0])
            hs[d, 1, :, pp * lanes:(pp + 1) * lanes] = _dot(ucat, w_ref[pp, d, 1])
        cat = lambda r: jnp.concatenate([a_ref[pp, d, r:r + 1, :] for pp in range(pairs_per_step)], axis=1)
        a_re, a_im, as_re, as_im = cat(0), cat(1), cat(2), cat(3)
        L = a_re.shape[1]

        h_re = jnp.zeros((1, L), F32)
        h_im = jnp.zeros((1, L), F32)
        order = range(n_ctx_chunks) if d == 0 else range(n_ctx_chunks - 1, -1, -1)
        for i in order:
            r = n_lat_chunks + i
            s_re, s_im = hs[d, 0, r:r + 1, :], hs[d, 1, r:r + 1, :]
            hs[d, 0, r:r + 1, :] = h_re
            hs[d, 1, r:r + 1, :] = h_im
            h_re, h_im = step(a_re, a_im, h_re, h_im, s_re, s_im)

        a8_re = jnp.broadcast_to(a_re, (seg, L))
        a8_im = jnp.broadcast_to(a_im, (seg, L))
        tile = lambda c: pl.ds(pl.multiple_of((c if d == 0 else cps - 1 - c) * seg, seg), seg)

        def local(c, carry):
            rows = tile(c)
            return step(a8_re, a8_im, carry[0], carry[1], hs[d, 0, rows, :], hs[d, 1, rows, :])

        e_re, e_im = lax.fori_loop(0, cps, local, (jnp.zeros((seg, L), F32), jnp.zeros((seg, L), F32)))

        hin_re = jnp.zeros((seg, L), F32)
        hin_im = jnp.zeros((seg, L), F32)
        ks = list(range(seg)) if d == 0 else list(range(seg - 1, -1, -1))
        for idx, k in enumerate(ks):
            if idx > 0:
                kp = ks[idx - 1]
                h_re, h_im = step(as_re, as_im, h_re, h_im, e_re[kp:kp + 1], e_im[kp:kp + 1])
            hin_re = jnp.where(sub == k, h_re, hin_re)
            hin_im = jnp.where(sub == k, h_im, hin_im)

        def final(c, carry):
            rows = tile(c)
            s_re, s_im = hs[d, 0, rows, :], hs[d, 1, rows, :]
            hs[d, 0, rows, :] = carry[0]
            hs[d, 1, rows, :] = carry[1]
            return step(a8_re, a8_im, carry[0], carry[1], s_re, s_im)

        lax.fori_loop(0, cps, final, (hin_re, hin_im))

    for gi in range(2 * pairs_per_step):
        pp, gg = gi // 2, gi % 2
        y = _dot(u_ref[gi], m_ref[gi])
        for d in range(2):
            cols = slice(gg * width, (gg + 1) * width)
            y = y + _dot(hs[d, 0, :, pp * lanes:(pp + 1) * lanes].astype(BF16), v_ref[pp, d, 0][:, cols])
            y = y + _dot(hs[d, 1, :, pp * lanes:(pp + 1) * lanes].astype(BF16), v_ref[pp, d, 1][:, cols])
        y_ref[gi] = y


def _ssm_core(u, m, w, v, a, n_lat_chunks, n_ctx_chunks):
    groups, rows, width = u.shape
    pairs_per_step = 4
    gps = 2 * pairs_per_step
    lanes = 2 * SSM_STATE
    n_dir = 2
    return pl.pallas_call(
        functools.partial(_ssm_core_kernel, n_lat_chunks=n_lat_chunks, n_ctx_chunks=n_ctx_chunks,
                          pairs_per_step=pairs_per_step),
        out_shape=jax.ShapeDtypeStruct((groups, rows, width), F32),
        grid=(groups // gps,),
        in_specs=[
            pl.BlockSpec((gps, rows, width), lambda q: (q, 0, 0)),
            pl.BlockSpec((gps, width, width), lambda q: (q, 0, 0)),
            pl.BlockSpec((pairs_per_step, n_dir, 2, 2 * width, lanes), lambda q: (q, 0, 0, 0, 0)),
            pl.BlockSpec((pairs_per_step, n_dir, 2, lanes, 2 * width), lambda q: (q, 0, 0, 0, 0)),
            pl.BlockSpec((pairs_per_step, n_dir, 4, lanes), lambda q: (q, 0, 0, 0)),
        ],
        out_specs=pl.BlockSpec((gps, rows, width), lambda q: (q, 0, 0)),
        scratch_shapes=[pltpu.VMEM((n_dir, 2, rows, pairs_per_step * lanes), F32)],
        compiler_params=_params(("arbitrary",)),
        name="ssm_core",
    )(u, m, w, v, a)


def _ssm_chunk_layout(u_lat, u_ctx):
    n = u_lat.shape[0]
    groups = u_lat.shape[1] // SSM_GROUP
    cps = n // SSM_CHUNK // SSM_SEGMENTS
    lat = u_lat.reshape(SSM_SEGMENTS, cps, SSM_CHUNK, groups, SSM_GROUP).transpose(3, 1, 0, 2, 4)
    lat = lat.reshape(groups, n // SSM_CHUNK, SSM_CHUNK * SSM_GROUP)
    ctx = u_ctx.reshape(-1, SSM_CHUNK, groups, SSM_GROUP).transpose(2, 0, 1, 3)
    ctx = ctx.reshape(groups, -1, SSM_CHUNK * SSM_GROUP)
    return jnp.concatenate([lat, ctx], axis=1)


def _ssm_token_layout(y, n):
    groups = y.shape[0]
    n_chunks = n // SSM_CHUNK
    cps = n_chunks // SSM_SEGMENTS
    lat = y[:, :n_chunks].reshape(groups, cps, SSM_SEGMENTS, SSM_CHUNK, SSM_GROUP).transpose(2, 1, 3, 0, 4)
    lat = lat.reshape(n, groups * SSM_GROUP)
    ctx = y[:, n_chunks:].reshape(groups, -1, SSM_CHUNK, SSM_GROUP).transpose(1, 2, 0, 3)
    ctx = ctx.reshape(-1, groups * SSM_GROUP)
    return lat, ctx


def _ssm_out_kernel(y_ref, u_ref, g_ref, d_ref, w_ref, b_ref, o_ref):
    y = y_ref[...] + d_ref[...] * u_ref[...].astype(F32)
    y = jax.nn.gelu(y, approximate=True)
    z = _dot(y.astype(BF16), w_ref[...]) + b_ref[...]
    o_ref[...] = (y * jax.nn.sigmoid(z) * g_ref[...].astype(F32)).astype(BF16)


def _ssm_out(y, p, d_skip, w_glu, b_glu, bm):
    n, width = y.shape
    blk = width // HEAD_DIM
    row = lambda i: (i, 0)
    const = lambda i: (0, 0)
    return pl.pallas_call(
        _ssm_out_kernel,
        out_shape=jax.ShapeDtypeStruct((n, width), BF16),
        grid=(n // bm,),
        in_specs=[
            pl.BlockSpec((bm, width), row),
            pl.BlockSpec((bm, width), lambda i: (i, SLOT_SU // blk)),
            pl.BlockSpec((bm, width), lambda i: (i, SLOT_SG // blk)),
            pl.BlockSpec((1, width), const),
            pl.BlockSpec((width, width), const),
            pl.BlockSpec((1, width), const),
        ],
        out_specs=pl.BlockSpec((bm, width), row),
        compiler_params=_params(("arbitrary",)),
        name="ssm_out",
    )(y, p, p, d_skip, w_glu, b_glu)


def _outproj_kernel(*refs, n_pieces, final):
    mix = refs[:n_pieces]
    ws = refs[n_pieces:2 * n_pieces]
    x_ref, gate_ref = refs[2 * n_pieces:2 * n_pieces + 2]
    rest = refs[2 * n_pieces + 2:]
    acc = _dot(mix[0][...], ws[0][...])
    for a, w in zip(mix[1:], ws[1:]):
        acc = acc + _dot(a[...], w[...])
    x = x_ref[...] + gate_ref[...] * acc
    if final:
        fg_ref, o_ref = rest
        x = x * lax.rsqrt(jnp.mean(x * x, axis=-1, keepdims=True) + NORM_EPS) * fg_ref[...]
    else:
        (o_ref,) = rest
    o_ref[...] = x


def _outproj(pieces, w_slabs, x, gate, final_gain, bm):
    n, d = x.shape
    row = lambda i: (i, 0)
    const = lambda i: (0, 0)
    in_specs = [pl.BlockSpec((bm, a.shape[1]), row) for a in pieces]
    in_specs += [pl.BlockSpec(w.shape, const) for w in w_slabs]
    in_specs += [pl.BlockSpec((bm, d), row), pl.BlockSpec((1, d), const)]
    args = list(pieces) + list(w_slabs) + [x, gate]
    final = final_gain is not None
    if final:
        in_specs.append(pl.BlockSpec((1, d), const))
        args.append(final_gain)
    return pl.pallas_call(
        functools.partial(_outproj_kernel, n_pieces=len(pieces), final=final),
        out_shape=jax.ShapeDtypeStruct((n, d), F32),
        grid=(n // bm,),
        in_specs=in_specs,
        out_specs=pl.BlockSpec((bm, d), row),
        compiler_params=_params(("arbitrary",)),
        name="outproj_final" if final else "outproj",
    )(*args)


def _rope_tables(n):
    pos = jnp.arange(n)
    half = HEAD_DIM // 4
    freqs = ROPE_THETA ** (-jnp.arange(half, dtype=F32) / half)
    ang_r = (pos // GRID_W).astype(F32)[:, None] * freqs[None, :]
    ang_c = (pos % GRID_W).astype(F32)[:, None] * freqs[None, :]
    cos = jnp.concatenate([jnp.cos(ang_r)] * 2 + [jnp.cos(ang_c)] * 2, axis=-1)
    sin = jnp.concatenate([-jnp.sin(ang_r), jnp.sin(ang_r), -jnp.sin(ang_c), jnp.sin(ang_c)], axis=-1)
    return cos, sin


def _reorder_w_in(w):
    hd = HEAD_DIM
    sizes = (6 * hd, 2 * hd, 2 * hd, 6 * hd, 6 * hd, 2 * hd, 2 * hd, 6 * hd, 4 * hd, 4 * hd)
    names = ("aq", "ak", "av", "ag", "bq", "bk", "bv", "bg", "su", "sg")
    parts, start = {}, 0
    for name, size in zip(names, sizes):
        parts[name] = w[:, start:start + size]
        start += size
    order = ("aq", "bq", "ak", "bk", "av", "bv", "su", "ag", "bg", "sg")
    return jnp.concatenate([parts[k] for k in order], axis=1).astype(BF16)


def kernel(x, c, ctx, c_ctx, w_ada, b_ada, norm_gain, w_in, a_q_gain, a_k_gain, b_sink, ssm_lambda_re,
           ssm_lambda_im, ssm_log_step, ssm_b_re, ssm_b_im, ssm_c_re, ssm_c_im, ssm_d, w_glu, b_glu, w_out,
           final_gain):
    depth, d = norm_gain.shape
    n = x.shape[1]
    n_ctx = ctx.shape[1]
    assert x.shape[0] == 1 and n % (SSM_CHUNK * SSM_SEGMENTS) == 0 and n % 512 == 0
    xs = x[0]
    cs = ctx[0]
    bm = 512
    attn_w = Q_HEADS * HEAD_DIM

    c_t = jnp.concatenate([c.reshape(d, 1), c_ctx.reshape(d, 1)], axis=1)
    mod = _modulation(c_t, w_ada, b_ada)
    tables = _rope_tables(n)
    n_lat_chunks = n // SSM_CHUNK
    n_ctx_chunks = n_ctx // SSM_CHUNK

    for layer in range(depth):
        last = layer == depth - 1
        shift, scale, gate = (mod[layer, :, i * d:(i + 1) * d] for i in range(3))
        w = _reorder_w_in(w_in[layer])
        gain = norm_gain[layer].reshape(1, d)
        qk_gain = jnp.stack([a_q_gain[layer], a_k_gain[layer]], axis=0)
        p_lat = _inproj(xs, shift[0:1], scale[0:1], gain, w, qk_gain, tables, bm)
        p_ctx = _inproj(cs, shift[1:2], scale[1:2], gain, w, qk_gain, None, n_ctx)

        mix_a = _attn_global(p_lat, p_ctx, bq=512, bk=_largest_block(n + n_ctx, 1280))
        mix_b = _attn_window(p_lat, p_ctx, b_sink[layer], bq=512)

        ops = _ssm_prep(ssm_lambda_re[layer], ssm_lambda_im[layer], ssm_log_step[layer], ssm_b_re[layer],
                        ssm_b_im[layer], ssm_c_re[layer], ssm_c_im[layer], n_lat_chunks // SSM_SEGMENTS)
        su = SLOT_SU * HEAD_DIM
        u = _ssm_chunk_layout(p_lat[:, su:su + 4 * HEAD_DIM], p_ctx[:, su:su + 4 * HEAD_DIM])
        y = _ssm_core(u, *ops, n_lat_chunks, n_ctx_chunks)
        y_lat, y_ctx = _ssm_token_layout(y, n)
        d_skip = ssm_d[layer].reshape(1, -1)
        wg = w_glu[layer].astype(BF16)
        bg = b_glu[layer].reshape(1, -1)
        mix_s = _ssm_out(y_lat, p_lat, d_skip, wg, bg, bm)

        wo = w_out[layer].astype(BF16)
        slabs = (wo[:attn_w], wo[attn_w:2 * attn_w], wo[2 * attn_w:])
        xs_new = _outproj((mix_a, mix_b, mix_s), slabs, xs, gate[0:1],
                          final_gain.reshape(1, d) if last else None, bm)
        if not last:
            sinks = jnp.stack([jnp.full_like(b_sink[layer], NEG_INF), b_sink[layer]], axis=0)
            mix_ab_c = _attn_ctx(p_ctx, sinks)
            mix_s_c = _ssm_out(y_ctx, p_ctx, d_skip, wg, bg, n_ctx)
            cs = _outproj((mix_ab_c, mix_s_c), (wo[:2 * attn_w], wo[2 * attn_w:]), cs, gate[1:2], None, n_ctx)
        xs = xs_new
    return xs[None]
```

```python
import functools
import math

import jax
import jax.numpy as jnp
from jax import lax
from jax.experimental import pallas as pl
from jax.experimental.pallas import tpu as pltpu

F32 = jnp.float32
BF16 = jnp.bfloat16

HEAD_DIM = 128
GRID_W = 64
Q_PER_KV = 3
KV_HEADS = 2
Q_HEADS = Q_PER_KV * KV_HEADS
WINDOW = 128
SSM_GROUP = 16
SSM_STATE = 64
SSM_T = 8
SSM_BLOCK_GROUPS = HEAD_DIM // SSM_GROUP
SSM_BLOCK_STATES = SSM_BLOCK_GROUPS * SSM_STATE
SCAN_ROWS = 8
SSM_ROW_BLOCK = 256
ONES_ROWS = 16
ROPE_THETA = 10000.0
NORM_EPS = 1e-6
NEG_INF = -1e30
LOG2_E = math.log2(math.e)
VMEM_LIMIT_V7X = 56 * 1024 * 1024

SLOT_AQ, SLOT_BQ, SLOT_AK, SLOT_BK, SLOT_AV, SLOT_BV = 0, 6, 12, 14, 16, 18
SLOT_SU, SLOT_AG, SLOT_BG, SLOT_SG = 20, 24, 30, 36
N_SLOTS = 40
PANEL_SLOTS = 8


def _params(sem, vmem=VMEM_LIMIT_V7X):
    return pltpu.CompilerParams(dimension_semantics=sem, vmem_limit_bytes=vmem)


def _largest_block(total, limit):
    return max(b for b in range(HEAD_DIM, limit + 1, HEAD_DIM) if total % b == 0)


def _dot(a, b):
    return jnp.dot(a, b, preferred_element_type=F32)


def _dot_nt(a, b):
    return lax.dot_general(a, b, (((1,), (1,)), ((), ())), preferred_element_type=F32)


def _silu(x):
    return x * jax.nn.sigmoid(x)


def _mod_kernel(ct_ref, w_ref, b_ref, o_ref):
    s = _silu(ct_ref[...])
    w = w_ref[...]
    b = b_ref[...]
    o_ref[0:1, :] = jnp.sum(s[:, 0:1] * w, axis=0, keepdims=True) + b
    o_ref[1:2, :] = jnp.sum(s[:, 1:2] * w, axis=0, keepdims=True) + b


def _modulation(c_t, w_ada, b_ada):
    depth, d, n3 = w_ada.shape
    tn = 512
    return pl.pallas_call(
        _mod_kernel,
        out_shape=jax.ShapeDtypeStruct((depth, 2, n3), F32),
        grid=(depth, n3 // tn),
        in_specs=[
            pl.BlockSpec((d, 2), lambda l, j: (0, 0)),
            pl.BlockSpec((None, d, tn), lambda l, j: (l, 0, j)),
            pl.BlockSpec((None, 1, tn), lambda l, j: (l, 0, j)),
        ],
        out_specs=pl.BlockSpec((None, 2, tn), lambda l, j: (l, 0, j)),
        compiler_params=_params(("arbitrary", "arbitrary")),
        name="adaln_modulation",
    )(c_t, w_ada, b_ada.reshape(depth, 1, n3))


def _slot_config(slot):
    scale = HEAD_DIM ** -0.5 * LOG2_E
    if slot < SLOT_BQ:
        return 0, True, scale, False
    if slot < SLOT_AK:
        return None, True, scale, False
    if slot < SLOT_BK:
        return 1, True, None, False
    if slot < SLOT_AV:
        return None, True, None, False
    if slot < SLOT_AG:
        return None, False, None, False
    return None, False, None, True


def _inproj_kernel(*refs, rope):
    if rope:
        x_ref, shift_ref, scale_ref, gain_ref, w_ref, qk_ref, cos_ref, sin_ref, o_ref, qt_ref, vt_ref, su_ref = refs
    else:
        x_ref, shift_ref, scale_ref, gain_ref, w_ref, qk_ref, o_ref, qt_ref, vt_ref, su_ref = refs
    x = x_ref[...]
    y = x * lax.rsqrt(jnp.mean(x * x, axis=-1, keepdims=True) + NORM_EPS) * gain_ref[...]
    h = (y * (1.0 + scale_ref[...]) + shift_ref[...]).astype(BF16)
    lane = lax.broadcasted_iota(jnp.int32, (1, HEAD_DIM), 1)
    first_half = (lane & 63) < 32
    bn = PANEL_SLOTS * HEAD_DIM
    vt_rows = HEAD_DIM + ONES_ROWS

    for panel in range(N_SLOTS // PANEL_SLOTS):
        acc = _dot(h, w_ref[:, panel * bn:(panel + 1) * bn])
        for k in range(PANEL_SLOTS):
            slot = panel * PANEL_SLOTS + k
            gain_row, rotary, scale, act = _slot_config(slot)
            t = acc[:, k * HEAD_DIM:(k + 1) * HEAD_DIM]
            if gain_row is not None:
                t = (t * lax.rsqrt(jnp.mean(t * t, axis=-1, keepdims=True) + NORM_EPS)
                     * qk_ref[gain_row:gain_row + 1, :])
            if rotary and rope:
                partner = jnp.where(first_half, pltpu.roll(t, 96, 1), pltpu.roll(t, 32, 1))
                t = t * cos_ref[...] + partner * sin_ref[...]
            if scale is not None:
                t = t * scale
            if act:
                t = _silu(t)
            o_ref[:, slot * HEAD_DIM:(slot + 1) * HEAD_DIM] = t.astype(BF16)
            if SLOT_AQ <= slot < SLOT_BQ:
                qt_ref[(slot - SLOT_AQ) * HEAD_DIM:(slot - SLOT_AQ + 1) * HEAD_DIM, :] = t.T.astype(BF16)
            if SLOT_AV <= slot < SLOT_BV:
                base = (slot - SLOT_AV) * vt_rows
                vt_ref[base:base + HEAD_DIM, :] = t.T.astype(BF16)
                vt_ref[base + HEAD_DIM:base + vt_rows, :] = jnp.ones((ONES_ROWS, t.shape[0]), BF16)
            if SLOT_SU <= slot < SLOT_AG:
                su_ref[:, (slot - SLOT_SU) * HEAD_DIM:(slot - SLOT_SU + 1) * HEAD_DIM] = t


def _inproj(x, shift, scale, gain, w, qk_gain, rope_tables, bm):
    n, d = x.shape
    width = w.shape[1]
    rope = rope_tables is not None
    row = lambda i: (i, 0)
    col = lambda i: (0, i)
    const = lambda i: (0, 0)
    in_specs = [
        pl.BlockSpec((bm, d), row),
        pl.BlockSpec((1, d), const),
        pl.BlockSpec((1, d), const),
        pl.BlockSpec((1, d), const),
        pl.BlockSpec((d, width), const, pipeline_mode=pl.Buffered(1)),
        pl.BlockSpec((2, HEAD_DIM), const),
    ]
    args = [x, shift, scale, gain, w, qk_gain]
    if rope:
        in_specs += [pl.BlockSpec((bm, HEAD_DIM), row), pl.BlockSpec((bm, HEAD_DIM), row)]
        args += list(rope_tables)
    qt_rows = Q_HEADS * HEAD_DIM
    vt_rows = KV_HEADS * (HEAD_DIM + ONES_ROWS)
    su_width = (SLOT_AG - SLOT_SU) * HEAD_DIM
    return pl.pallas_call(
        functools.partial(_inproj_kernel, rope=rope),
        out_shape=(
            jax.ShapeDtypeStruct((n, width), BF16),
            jax.ShapeDtypeStruct((qt_rows, n), BF16),
            jax.ShapeDtypeStruct((vt_rows, n), BF16),
            jax.ShapeDtypeStruct((n, su_width), F32),
        ),
        grid=(n // bm,),
        in_specs=in_specs,
        out_specs=(
            pl.BlockSpec((bm, width), row),
            pl.BlockSpec((qt_rows, bm), col),
            pl.BlockSpec((vt_rows, bm), col),
            pl.BlockSpec((bm, su_width), row),
        ),
        compiler_params=_params(("arbitrary",)),
        name="inproj_rope" if rope else "inproj_ctx",
    )(*args)


def _stack_heads(ref, rows=None):
    sl = slice(None) if rows is None else rows
    return jnp.concatenate([ref[sl, g * HEAD_DIM:(g + 1) * HEAD_DIM] for g in range(Q_PER_KV)], axis=0)


def _attn_global_kernel(qt_ref, qn_ref, k0_ref, kn_ref, vt_ref, g_ref, o_ref, qa_s, s_s, mb_s, m_s, acc_s,
                        *, bq, nk, tq):
    qi = pl.program_id(1)
    ki = pl.program_id(2)
    tiles = [(g, c) for g in range(Q_PER_KV) for c in range(bq // tq)]

    def lanes(g, c):
        return slice(g * bq + c * tq, g * bq + (c + 1) * tq)

    def scores(k_ref, g, c):
        s = _dot(k_ref[...], qa_s[g * HEAD_DIM:(g + 1) * HEAD_DIM, c * tq:(c + 1) * tq])
        s_s[:, lanes(g, c)] = s
        mb_s[:, lanes(g, c)] = jnp.max(s, axis=0, keepdims=True)

    @pl.when((qi == 0) & (ki == 0))
    def _():
        qa_s[...] = qt_ref[...]
        for g, c in tiles:
            scores(k0_ref, g, c)

    @pl.when(ki == 0)
    def _():
        m_s[...] = jnp.full(m_s.shape, NEG_INF, F32)
        acc_s[...] = jnp.zeros(acc_s.shape, F32)

    @pl.when(ki == nk - 1)
    def _():
        qa_s[...] = qn_ref[...]

    vt = vt_ref[...]
    for g, c in tiles:
        cols = lanes(g, c)
        m_prev = m_s[:, cols]
        m_new = jnp.maximum(m_prev, mb_s[:, cols])
        alpha = jnp.exp2(m_prev - m_new)
        p = jnp.exp2((s_s[:, cols] - m_new).astype(BF16))
        acc_s[:, cols] = alpha * acc_s[:, cols] + _dot(vt, p)
        m_s[:, cols] = m_new
        scores(kn_ref, g, c)

    @pl.when(ki == nk - 1)
    def _():
        for g in range(Q_PER_KV):
            cols = slice(g * bq, (g + 1) * bq)
            out_t = acc_s[0:HEAD_DIM, cols] / acc_s[HEAD_DIM:HEAD_DIM + 1, cols]
            gate = g_ref[:, g * HEAD_DIM:(g + 1) * HEAD_DIM].astype(F32)
            o_ref[:, g * HEAD_DIM:(g + 1) * HEAD_DIM] = (out_t.T * gate).astype(BF16)


def _attn_global(p_lat, p_ctx, qt, vt_lat, vt_ctx, bq, bk):
    n = p_lat.shape[0]
    hd = HEAD_DIM
    qw = Q_PER_KV * hd
    ones_rows = ONES_ROWS
    k_all = jnp.concatenate([p_lat[:, SLOT_AK * hd:SLOT_BK * hd], p_ctx[:, SLOT_AK * hd:SLOT_BK * hd]], axis=0)
    vt_ext = jnp.concatenate([vt_lat, vt_ctx], axis=1)
    n_kv = k_all.shape[0]
    assert n_kv % bk == 0 and n % bq == 0
    nk = n_kv // bk
    nq = n // bq
    width = Q_PER_KV * bq
    return pl.pallas_call(
        functools.partial(_attn_global_kernel, bq=bq, nk=nk, tq=512),
        out_shape=jax.ShapeDtypeStruct((n, Q_HEADS * hd), BF16),
        grid=(KV_HEADS, nq, nk),
        in_specs=[
            pl.BlockSpec((qw, bq), lambda h, i, k: (h, i)),
            pl.BlockSpec((qw, bq), lambda h, i, k: (h, jnp.minimum(i + 1, nq - 1))),
            pl.BlockSpec((bk, hd), lambda h, i, k: (0, h)),
            pl.BlockSpec((bk, hd), lambda h, i, k: ((k + 1) % nk, h)),
            pl.BlockSpec((hd + ones_rows, bk), lambda h, i, k: (h, k)),
            pl.BlockSpec((bq, qw), lambda h, i, k: (i, SLOT_AG // Q_PER_KV + h)),
        ],
        out_specs=pl.BlockSpec((bq, qw), lambda h, i, k: (i, h)),
        scratch_shapes=[
            pltpu.VMEM((qw, bq), BF16),
            pltpu.VMEM((bk, width), F32),
            pltpu.VMEM((1, width), F32),
            pltpu.VMEM((1, width), F32),
            pltpu.VMEM((hd + ones_rows, width), F32),
        ],
        compiler_params=_params(("arbitrary", "arbitrary", "arbitrary")),
        name="attn_global",
    )(qt, qt, k_all, k_all, vt_ext, p_lat)


def _sink_column(sink_ref, base, rows_per_head, t=None):
    row = lax.broadcasted_iota(jnp.int32, (Q_PER_KV * rows_per_head, 1), 0)
    get = (lambda g: sink_ref[base + g]) if t is None else (lambda g: sink_ref[t, base + g])
    get = functools.partial(lambda f, g: f(g) * LOG2_E, get)
    return jnp.where(row < rows_per_head, get(0), jnp.where(row < 2 * rows_per_head, get(1), get(2)))


def _attn_window_kernel(sink_ref, q_ref, kp_ref, km_ref, kn_ref, vp_ref, vm_ref, vn_ref, kc_ref, vc_ref,
                        g_ref, o_ref, *, bq, n):
    h = pl.program_id(0)
    qi = pl.program_id(1)
    kcat = jnp.concatenate([kp_ref[...], km_ref[...], kn_ref[...]], axis=0)
    vcat = jnp.concatenate([vp_ref[...], vm_ref[...], vn_ref[...]], axis=0)
    kc = kc_ref[...]
    vc = vc_ref[...]
    span = 3 * WINDOW
    rows = Q_PER_KV * WINDOW
    r = lax.broadcasted_iota(jnp.int32, (rows, span), 0) & (WINDOW - 1)
    cidx = lax.broadcasted_iota(jnp.int32, (rows, span), 1)
    rel = cidx - r
    band = (rel >= 0) & (rel <= 2 * WINDOW)
    sink = _sink_column(sink_ref, h * Q_PER_KV, WINDOW)
    for sb in range(bq // WINDOW):
        q3 = _stack_heads(q_ref, slice(sb * WINDOW, (sb + 1) * WINDOW))
        kw = kcat[sb * WINDOW:sb * WINDOW + span]
        vw = vcat[sb * WINDOW:sb * WINDOW + span]
        key_pos = qi * bq + (sb - 1) * WINDOW + cidx
        valid = band & (key_pos >= 0) & (key_pos < n)
        s = jnp.where(valid, _dot_nt(q3, kw), NEG_INF)
        sc = _dot_nt(q3, kc)
        m = jnp.maximum(jnp.maximum(jnp.max(s, axis=-1, keepdims=True), jnp.max(sc, axis=-1, keepdims=True)), sink)
        p = jnp.exp2(s - m)
        pc = jnp.exp2(sc - m)
        den = jnp.sum(p, axis=-1, keepdims=True) + jnp.sum(pc, axis=-1, keepdims=True) + jnp.exp2(sink - m)
        out = (_dot(p.astype(BF16), vw) + _dot(pc.astype(BF16), vc)) / den
        for g in range(Q_PER_KV):
            gate = g_ref[sb * WINDOW:(sb + 1) * WINDOW, g * HEAD_DIM:(g + 1) * HEAD_DIM].astype(F32)
            o_ref[sb * WINDOW:(sb + 1) * WINDOW, g * HEAD_DIM:(g + 1) * HEAD_DIM] = (
                out[g * WINDOW:(g + 1) * WINDOW] * gate).astype(BF16)


def _attn_window(p_lat, p_ctx, sink, bq):
    n = p_lat.shape[0]
    n_ctx = p_ctx.shape[0]
    qw = Q_PER_KV * HEAD_DIM
    per = bq // WINDOW
    last = n // WINDOW - 1
    prev_map = lambda slot: (lambda h, i: (jnp.maximum(i * per - 1, 0), slot + h))
    main_map = lambda slot: (lambda h, i: (i, slot + h))
    next_map = lambda slot: (lambda h, i: (jnp.minimum((i + 1) * per, last), slot + h))
    return pl.pallas_call(
        functools.partial(_attn_window_kernel, bq=bq, n=n),
        out_shape=jax.ShapeDtypeStruct((n, Q_HEADS * HEAD_DIM), BF16),
        grid=(KV_HEADS, n // bq),
        in_specs=[
            pl.BlockSpec(memory_space=pltpu.SMEM),
            pl.BlockSpec((bq, qw), lambda h, i: (i, SLOT_BQ // Q_PER_KV + h)),
            pl.BlockSpec((WINDOW, HEAD_DIM), prev_map(SLOT_BK)),
            pl.BlockSpec((bq, HEAD_DIM), main_map(SLOT_BK)),
            pl.BlockSpec((WINDOW, HEAD_DIM), next_map(SLOT_BK)),
            pl.BlockSpec((WINDOW, HEAD_DIM), prev_map(SLOT_BV)),
            pl.BlockSpec((bq, HEAD_DIM), main_map(SLOT_BV)),
            pl.BlockSpec((WINDOW, HEAD_DIM), next_map(SLOT_BV)),
            pl.BlockSpec((n_ctx, HEAD_DIM), lambda h, i: (0, SLOT_BK + h)),
            pl.BlockSpec((n_ctx, HEAD_DIM), lambda h, i: (0, SLOT_BV + h)),
            pl.BlockSpec((bq, qw), lambda h, i: (i, SLOT_BG // Q_PER_KV + h)),
        ],
        out_specs=pl.BlockSpec((bq, qw), lambda h, i: (i, h)),
        compiler_params=_params(("arbitrary", "arbitrary")),
        name="attn_window",
    )(sink, p_lat, p_lat, p_lat, p_lat, p_lat, p_lat, p_lat, p_ctx, p_ctx, p_lat)


def _attn_ctx_kernel(sink_ref, q_ref, k_ref, v_ref, g_ref, o_ref, *, n_ctx):
    idx = pl.program_id(0)
    t = idx // KV_HEADS
    h = idx % KV_HEADS
    q3 = _stack_heads(q_ref)
    s = _dot_nt(q3, k_ref[...])
    sink = _sink_column(sink_ref, h * Q_PER_KV, n_ctx, t=t)
    m = jnp.maximum(jnp.max(s, axis=-1, keepdims=True), sink)
    p = jnp.exp2(s - m)
    den = jnp.sum(p, axis=-1, keepdims=True) + jnp.exp2(sink - m)
    out = _dot(p.astype(BF16), v_ref[...]) / den
    for g in range(Q_PER_KV):
        gate = g_ref[:, g * HEAD_DIM:(g + 1) * HEAD_DIM].astype(F32)
        o_ref[:, g * HEAD_DIM:(g + 1) * HEAD_DIM] = (out[g * n_ctx:(g + 1) * n_ctx] * gate).astype(BF16)


def _attn_ctx(p_ctx, sinks):
    n_ctx = p_ctx.shape[0]
    qw = Q_PER_KV * HEAD_DIM
    return pl.pallas_call(
        functools.partial(_attn_ctx_kernel, n_ctx=n_ctx),
        out_shape=jax.ShapeDtypeStruct((n_ctx, 2 * Q_HEADS * HEAD_DIM), BF16),
        grid=(2 * KV_HEADS,),
        in_specs=[
            pl.BlockSpec(memory_space=pltpu.SMEM),
            pl.BlockSpec((n_ctx, qw), lambda i: (0, i)),
            pl.BlockSpec((n_ctx, HEAD_DIM), lambda i: (0, SLOT_AK + i)),
            pl.BlockSpec((n_ctx, HEAD_DIM), lambda i: (0, SLOT_AV + i)),
            pl.BlockSpec((n_ctx, qw), lambda i: (0, SLOT_AG // Q_PER_KV + i)),
        ],
        out_specs=pl.BlockSpec((n_ctx, qw), lambda i: (0, i)),
        compiler_params=_params(("arbitrary",)),
        name="attn_ctx",
    )(sinks, p_ctx, p_ctx, p_ctx, p_ctx)


def _complex_mul(a_re, a_im, b_re, b_im):
    return a_re * b_re - a_im * b_im, a_re * b_im + a_im * b_re


def _dot_split(a, b):
    a_hi = a.astype(BF16)
    b_hi = b.astype(BF16)
    a_lo = (a - a_hi.astype(F32)).astype(BF16)
    b_lo = (b - b_hi.astype(F32)).astype(BF16)
    return _dot(a_hi, b_hi) + _dot(a_hi, b_lo) + _dot(a_lo, b_hi)


def _ssm_prep_kernel(lam_r_ref, bd_b_ref, bd_c_ref, m_ref, w_ref, v_ref, sc_ref):
    T = SSM_T
    row8 = lax.broadcasted_iota(jnp.int32, (SCAN_ROWS, 1), 0)
    n_exp = T + 1 + 3 + SCAN_ROWS
    n_col = -(-(T + 1) // SCAN_ROWS) * SCAN_ROWS
    erow = lax.broadcasted_iota(jnp.int32, (n_exp, 1), 0)
    lag_kernels = []
    for d in range(2):
        lam_re, lam_im = lam_r_ref[d, 0:1, :], lam_r_ref[d, 1:2, :]
        step = jnp.exp(lam_r_ref[d, 2:3, :])
        dist = erow - (T + 4)
        dist = dist if d == 0 else SCAN_ROWS - 1 - dist
        expo = jnp.where(erow <= T, erow,
                         jnp.where(erow == T + 1, 2 * T,
                                   jnp.where(erow == T + 2, 4 * T,
                                             jnp.where(erow == T + 3, 8 * T, T * dist)))).astype(F32)
        mag = jnp.exp(lam_re * step * expo)
        ang = lam_im * step * expo
        p_re, p_im = mag * jnp.cos(ang), mag * jnp.sin(ang)
        pc_re, pc_im = p_re[0:n_col].T, p_im[0:n_col].T

        nr, ni = p_re[1:2] - 1.0, p_im[1:2]
        den = lam_re * lam_re + lam_im * lam_im
        q_re = (nr * lam_re + ni * lam_im) / den
        q_im = (ni * lam_re - nr * lam_im) / den
        bb_re, bb_im = _complex_mul(bd_b_ref[d, 0], bd_b_ref[d, 1], q_re, q_im)
        c_re, c_im = bd_c_ref[d, 0], bd_c_ref[d, 1]

        def w_rows(e):
            g_re, g_im = _complex_mul(bb_re, bb_im, p_re[e:e + 1], p_im[e:e + 1])
            return jnp.concatenate([g_re, g_im], axis=1)

        def v_cols(e):
            g_re, g_im = _complex_mul(c_re, c_im, pc_re[:, e:e + 1], pc_im[:, e:e + 1])
            return jnp.concatenate([g_re, -g_im], axis=0)

        w = jnp.concatenate([w_rows(T - 1 - t if d == 0 else t) for t in range(T)], axis=0)
        w_ref[d] = w.astype(BF16)
        v_ref[d] = jnp.concatenate([v_cols(t + 1 if d == 0 else T - t) for t in range(T)], axis=1).astype(BF16)
        lags = _dot_split(w, v_cols(0))
        lag_of_block = [(T - 1 - t if d == 0 else t) for t in range(T)]
        lag_kernels.append({lag: lags[t * HEAD_DIM:(t + 1) * HEAD_DIM] for t, lag in enumerate(lag_of_block)})

        ahead = (lambda k: row8 >= k) if d == 0 else (lambda k: row8 <= SCAN_ROWS - 1 - k)
        rows_of = {1: T, 2: T + 1, 4: T + 2, 8: T + 3}
        parts_re, parts_im = [], []
        for k in (1, 2, 4):
            r = rows_of[k]
            parts_re.append(jnp.where(ahead(k), p_re[r:r + 1], 0.0))
            parts_im.append(jnp.where(ahead(k), p_im[r:r + 1], 0.0))
        parts_re.append(p_re[T + 4:T + 4 + SCAN_ROWS])
        parts_im.append(p_im[T + 4:T + 4 + SCAN_ROWS])
        r = rows_of[8]
        parts_re.append(jnp.broadcast_to(p_re[r:r + 1], (SCAN_ROWS, p_re.shape[1])))
        parts_im.append(jnp.broadcast_to(p_im[r:r + 1], (SCAN_ROWS, p_im.shape[1])))
        sc_ref[d, 0] = jnp.concatenate(parts_re, axis=0)
        sc_ref[d, 1] = jnp.concatenate(parts_im, axis=0)

    fwd, bwd = lag_kernels
    for t in range(T):
        blocks = []
        for t2 in range(T):
            if t2 > t:
                blocks.append(fwd[t2 - t])
            elif t2 < t:
                blocks.append(bwd[t - t2])
            else:
                blocks.append(fwd[0] + bwd[0])
        m_ref[t * HEAD_DIM:(t + 1) * HEAD_DIM, :] = jnp.concatenate(blocks, axis=1).astype(BF16)


def _ssm_prep(lam_re, lam_im, log_step, b_re, b_im, c_re, c_im):
    n_dir, groups, state = lam_re.shape
    bg = SSM_BLOCK_GROUPS
    nb = groups // bg
    ns = SSM_BLOCK_STATES
    width = SSM_T * HEAD_DIM
    step = jnp.broadcast_to(log_step[:, :, None], lam_re.shape)
    rows = jnp.stack([lam_re, lam_im, step], axis=1)
    lam_r = rows.reshape(n_dir, 3, nb, ns).transpose(2, 0, 1, 3)
    eye = jnp.eye(bg, dtype=F32)

    def b_layout(b):
        bt = b.reshape(n_dir, nb, bg, state, SSM_GROUP)
        return jnp.einsum('dbgpj,gh->bdgjhp', bt, eye).reshape(nb, n_dir, HEAD_DIM, ns)

    def c_layout(c):
        ct = c.reshape(n_dir, nb, bg, SSM_GROUP, state)
        return jnp.einsum('dbhip,gh->bdhpgi', ct, eye).reshape(nb, n_dir, ns, HEAD_DIM)

    bd_b = jnp.stack([b_layout(b_re), b_layout(b_im)], axis=2)
    bd_c = jnp.stack([c_layout(c_re), c_layout(c_im)], axis=2)
    return pl.pallas_call(
        _ssm_prep_kernel,
        out_shape=(
            jax.ShapeDtypeStruct((nb, width, width), BF16),
            jax.ShapeDtypeStruct((nb, n_dir, width, 2 * ns), BF16),
            jax.ShapeDtypeStruct((nb, n_dir, 2 * ns, width), BF16),
            jax.ShapeDtypeStruct((nb, n_dir, 2, 5 * SCAN_ROWS, ns), F32),
        ),
        grid=(nb,),
        in_specs=[
            pl.BlockSpec((None, n_dir, 3, ns), lambda b: (b, 0, 0, 0)),
            pl.BlockSpec((None, n_dir, 2, HEAD_DIM, ns), lambda b: (b, 0, 0, 0, 0)),
            pl.BlockSpec((None, n_dir, 2, ns, HEAD_DIM), lambda b: (b, 0, 0, 0, 0)),
        ],
        out_specs=(
            pl.BlockSpec((None, width, width), lambda b: (b, 0, 0)),
            pl.BlockSpec((None, n_dir, width, 2 * ns), lambda b: (b, 0, 0, 0)),
            pl.BlockSpec((None, n_dir, 2 * ns, width), lambda b: (b, 0, 0, 0)),
            pl.BlockSpec((None, n_dir, 2, 5 * SCAN_ROWS, ns), lambda b: (b, 0, 0, 0, 0)),
        ),
        compiler_params=_params(("arbitrary",)),
        name="ssm_prep",
    )(lam_r, bd_b, bd_c)


def _chunk_rows(u_ref, rb):
    return jnp.concatenate([u_ref[pl.ds(t, rb, stride=SSM_T), :] for t in range(SSM_T)], axis=1).astype(BF16)


def _ssm_states_kernel(uf_ref, ub_ref, w_ref, sc_ref, cin_ref, hf_ref, hb_ref, cout_ref, s_s, carry_s, *, rb, nrb):
    step = pl.program_id(1)
    ns = SSM_BLOCK_STATES
    row8 = lax.broadcasted_iota(jnp.int32, (SCAN_ROWS, 1), 0)

    @pl.when(step == 0)
    def _():
        carry_s[...] = cin_ref[...]

    for d, (u_ref, h_ref) in enumerate(((uf_ref, hf_ref), (ub_ref, hb_ref))):
        s_s[d] = _dot(_chunk_rows(u_ref, rb), w_ref[d])
        inner = (row8 >= 1) if d == 0 else (row8 <= SCAN_ROWS - 2)
        em = jnp.where(inner, 1.0, 0.0).astype(F32)
        consts = [(sc_ref[d, 0, i * SCAN_ROWS:(i + 1) * SCAN_ROWS, :], sc_ref[d, 1, i * SCAN_ROWS:(i + 1) * SCAN_ROWS, :])
                  for i in range(5)]
        n_groups = rb // SCAN_ROWS

        def body(g, carry, d=d, em=em, consts=consts):
            c_re, c_im = carry
            gi = g if d == 0 else n_groups - 1 - g
            rows = pl.ds(pl.multiple_of(gi * SCAN_ROWS, SCAN_ROWS), SCAN_ROWS)
            p_re = s_s[d, rows, 0:ns]
            p_im = s_s[d, rows, ns:2 * ns]
            for i, k in enumerate((1, 2, 4)):
                shift = k if d == 0 else SCAN_ROWS - k
                m_re, m_im = _complex_mul(consts[i][0], consts[i][1], pltpu.roll(p_re, shift, 0), pltpu.roll(p_im, shift, 0))
                p_re, p_im = p_re + m_re, p_im + m_im
            shift = 1 if d == 0 else SCAN_ROWS - 1
            k_re, k_im = _complex_mul(consts[3][0], consts[3][1], c_re, c_im)
            s_s[d, rows, 0:ns] = em * pltpu.roll(p_re, shift, 0) + k_re
            s_s[d, rows, ns:2 * ns] = em * pltpu.roll(p_im, shift, 0) + k_im
            last = SCAN_ROWS - 1 if d == 0 else 0
            n_re, n_im = _complex_mul(consts[4][0][0:1], consts[4][1][0:1], c_re, c_im)
            return p_re[last:last + 1] + n_re, p_im[last:last + 1] + n_im

        c_re, c_im = lax.fori_loop(0, n_groups, body, (carry_s[d, 0, 0:1, :], carry_s[d, 1, 0:1, :]))
        carry_s[d, 0, 0:1, :] = c_re
        carry_s[d, 1, 0:1, :] = c_im
        h_ref[...] = s_s[d].astype(BF16)

    @pl.when(step == nrb - 1)
    def _():
        cout_ref[...] = carry_s[...]


def _ssm_states(u, w, sc, carry_in):
    n_tok, width = u.shape
    nb = width // HEAD_DIM
    n_chunks = n_tok // SSM_T
    rb = min(SSM_ROW_BLOCK, n_chunks)
    nrb = n_chunks // rb
    ns = SSM_BLOCK_STATES
    h_shape = jax.ShapeDtypeStruct((n_chunks, nb * 2 * ns), BF16)
    return pl.pallas_call(
        functools.partial(_ssm_states_kernel, rb=rb, nrb=nrb),
        out_shape=(h_shape, h_shape, jax.ShapeDtypeStruct(carry_in.shape, F32)),
        grid=(nb, nrb),
        in_specs=[
            pl.BlockSpec((rb * SSM_T, HEAD_DIM), lambda b, r: (r, b)),
            pl.BlockSpec((rb * SSM_T, HEAD_DIM), lambda b, r: (nrb - 1 - r, b)),
            pl.BlockSpec((None, 2, SSM_T * HEAD_DIM, 2 * ns), lambda b, r: (b, 0, 0, 0)),
            pl.BlockSpec((None, 2, 2, 5 * SCAN_ROWS, ns), lambda b, r: (b, 0, 0, 0, 0)),
            pl.BlockSpec((None, 2, 2, SCAN_ROWS, ns), lambda b, r: (b, 0, 0, 0, 0)),
        ],
        out_specs=(
            pl.BlockSpec((rb, 2 * ns), lambda b, r: (r, b)),
            pl.BlockSpec((rb, 2 * ns), lambda b, r: (nrb - 1 - r, b)),
            pl.BlockSpec((None, 2, 2, SCAN_ROWS, ns), lambda b, r: (b, 0, 0, 0, 0)),
        ),
        scratch_shapes=[pltpu.VMEM((2, rb, 2 * ns), F32), pltpu.VMEM((2, 2, SCAN_ROWS, ns), F32)],
        compiler_params=_params(("arbitrary", "arbitrary")),
        name="ssm_states",
    )(u, u, w, sc, carry_in)


def _ssm_y_kernel(u_ref, hf_ref, hb_ref, m_ref, v_ref, y_ref, *, rb):
    ycat = (_dot(_chunk_rows(u_ref, rb), m_ref[...]) + _dot(hf_ref[...], v_ref[0]) + _dot(hb_ref[...], v_ref[1]))
    for t in range(SSM_T):
        y_ref[pl.ds(t, rb, stride=SSM_T), :] = ycat[:, t * HEAD_DIM:(t + 1) * HEAD_DIM]


def _ssm_y(u, hf, hb, m, v):
    n_tok, width = u.shape
    nb = width // HEAD_DIM
    n_chunks = n_tok // SSM_T
    rb = min(SSM_ROW_BLOCK, n_chunks)
    ns = SSM_BLOCK_STATES
    cw = SSM_T * HEAD_DIM
    return pl.pallas_call(
        functools.partial(_ssm_y_kernel, rb=rb),
        out_shape=jax.ShapeDtypeStruct((n_tok, width), F32),
        grid=(nb, n_chunks // rb),
        in_specs=[
            pl.BlockSpec((rb * SSM_T, HEAD_DIM), lambda b, r: (r, b)),
            pl.BlockSpec((rb, 2 * ns), lambda b, r: (r, b)),
            pl.BlockSpec((rb, 2 * ns), lambda b, r: (r, b)),
            pl.BlockSpec((None, cw, cw), lambda b, r: (b, 0, 0)),
            pl.BlockSpec((None, 2, 2 * ns, cw), lambda b, r: (b, 0, 0, 0)),
        ],
        out_specs=pl.BlockSpec((rb * SSM_T, HEAD_DIM), lambda b, r: (r, b)),
        compiler_params=_params(("arbitrary", "arbitrary")),
        name="ssm_y",
    )(u, hf, hb, m, v)


def _ssm_out_kernel(y_ref, u_ref, g_ref, d_ref, w_ref, b_ref, o_ref):
    y = y_ref[...] + d_ref[...] * u_ref[...].astype(F32)
    y = jax.nn.gelu(y, approximate=True)
    z = _dot(y.astype(BF16), w_ref[...]) + b_ref[...]
    o_ref[...] = (y * jax.nn.sigmoid(z) * g_ref[...].astype(F32)).astype(BF16)


def _ssm_out(y, p, d_skip, w_glu, b_glu, bm):
    n, width = y.shape
    blk = width // HEAD_DIM
    row = lambda i: (i, 0)
    const = lambda i: (0, 0)
    return pl.pallas_call(
        _ssm_out_kernel,
        out_shape=jax.ShapeDtypeStruct((n, width), BF16),
        grid=(n // bm,),
        in_specs=[
            pl.BlockSpec((bm, width), row),
            pl.BlockSpec((bm, width), lambda i: (i, SLOT_SU // blk)),
            pl.BlockSpec((bm, width), lambda i: (i, SLOT_SG // blk)),
            pl.BlockSpec((1, width), const),
            pl.BlockSpec((width, width), const),
            pl.BlockSpec((1, width), const),
        ],
        out_specs=pl.BlockSpec((bm, width), row),
        compiler_params=_params(("arbitrary",)),
        name="ssm_out",
    )(y, p, p, d_skip, w_glu, b_glu)


def _outproj_kernel(*refs, n_pieces, final):
    mix = refs[:n_pieces]
    ws = refs[n_pieces:2 * n_pieces]
    x_ref, gate_ref = refs[2 * n_pieces:2 * n_pieces + 2]
    rest = refs[2 * n_pieces + 2:]
    acc = _dot(mix[0][...], ws[0][...])
    for a, w in zip(mix[1:], ws[1:]):
        acc = acc + _dot(a[...], w[...])
    x = x_ref[...] + gate_ref[...] * acc
    if final:
        fg_ref, o_ref = rest
        x = x * lax.rsqrt(jnp.mean(x * x, axis=-1, keepdims=True) + NORM_EPS) * fg_ref[...]
    else:
        (o_ref,) = rest
    o_ref[...] = x


def _outproj(pieces, w_slabs, x, gate, final_gain, bm):
    n, d = x.shape
    row = lambda i: (i, 0)
    const = lambda i: (0, 0)
    in_specs = [pl.BlockSpec((bm, a.shape[1]), row) for a in pieces]
    in_specs += [pl.BlockSpec(w.shape, const) for w in w_slabs]
    in_specs += [pl.BlockSpec((bm, d), row), pl.BlockSpec((1, d), const)]
    args = list(pieces) + list(w_slabs) + [x, gate]
    final = final_gain is not None
    if final:
        in_specs.append(pl.BlockSpec((1, d), const))
        args.append(final_gain)
    return pl.pallas_call(
        functools.partial(_outproj_kernel, n_pieces=len(pieces), final=final),
        out_shape=jax.ShapeDtypeStruct((n, d), F32),
        grid=(n // bm,),
        in_specs=in_specs,
        out_specs=pl.BlockSpec((bm, d), row),
        compiler_params=_params(("arbitrary",)),
        name="outproj_final" if final else "outproj",
    )(*args)


def _rope_tables(n):
    pos = jnp.arange(n)
    half = HEAD_DIM // 4
    freqs = ROPE_THETA ** (-jnp.arange(half, dtype=F32) / half)
    ang_r = (pos // GRID_W).astype(F32)[:, None] * freqs[None, :]
    ang_c = (pos % GRID_W).astype(F32)[:, None] * freqs[None, :]
    cos = jnp.concatenate([jnp.cos(ang_r)] * 2 + [jnp.cos(ang_c)] * 2, axis=-1)
    sin = jnp.concatenate([-jnp.sin(ang_r), jnp.sin(ang_r), -jnp.sin(ang_c), jnp.sin(ang_c)], axis=-1)
    return cos, sin


def _reorder_w_in(w):
    hd = HEAD_DIM
    sizes = (6 * hd, 2 * hd, 2 * hd, 6 * hd, 6 * hd, 2 * hd, 2 * hd, 6 * hd, 4 * hd, 4 * hd)
    names = ("aq", "ak", "av", "ag", "bq", "bk", "bv", "bg", "su", "sg")
    parts, start = {}, 0
    for name, size in zip(names, sizes):
        parts[name] = w[:, start:start + size]
        start += size
    order = ("aq", "bq", "ak", "bk", "av", "bv", "su", "ag", "bg", "sg")
    return jnp.concatenate([parts[k] for k in order], axis=1).astype(BF16)


def kernel(x, c, ctx, c_ctx, w_ada, b_ada, norm_gain, w_in, a_q_gain, a_k_gain, b_sink, ssm_lambda_re,
           ssm_lambda_im, ssm_log_step, ssm_b_re, ssm_b_im, ssm_c_re, ssm_c_im, ssm_d, w_glu, b_glu, w_out,
           final_gain):
    depth, d = norm_gain.shape
    n = x.shape[1]
    n_ctx = ctx.shape[1]
    assert x.shape[0] == 1 and n % 512 == 0 and n_ctx % (SSM_T * SCAN_ROWS) == 0
    xs = x[0]
    cs = ctx[0]
    bm = 512
    attn_w = Q_HEADS * HEAD_DIM

    c_t = jnp.concatenate([c.reshape(d, 1), c_ctx.reshape(d, 1)], axis=1)
    mod = _modulation(c_t, w_ada, b_ada)
    tables = _rope_tables(n)

    for layer in range(depth):
        last = layer == depth - 1
        shift, scale, gate = (mod[layer, :, i * d:(i + 1) * d] for i in range(3))
        w = _reorder_w_in(w_in[layer])
        gain = norm_gain[layer].reshape(1, d)
        qk_gain = jnp.stack([a_q_gain[layer], a_k_gain[layer]], axis=0)
        p_lat, qt_lat, vt_lat, su_lat = _inproj(xs, shift[0:1], scale[0:1], gain, w, qk_gain, tables, bm)
        p_ctx, _, vt_ctx, su_ctx = _inproj(cs, shift[1:2], scale[1:2], gain, w, qk_gain, None, n_ctx)

        mix_a = _attn_global(p_lat, p_ctx, qt_lat, vt_lat, vt_ctx, bq=_largest_block(n, 1024),
                             bk=_largest_block(n + n_ctx, 1280))
        mix_b = _attn_window(p_lat, p_ctx, b_sink[layer], bq=512)

        m_op, w_op, v_op, scan_c = _ssm_prep(ssm_lambda_re[layer], ssm_lambda_im[layer], ssm_log_step[layer],
                                            ssm_b_re[layer], ssm_b_im[layer], ssm_c_re[layer], ssm_c_im[layer])
        no_state = jnp.zeros((su_lat.shape[1] // HEAD_DIM, 2, 2, SCAN_ROWS, SSM_BLOCK_STATES), F32)
        hf_ctx, hb_ctx, ctx_state = _ssm_states(su_ctx, w_op, scan_c, no_state)
        hf_lat, hb_lat, _ = _ssm_states(su_lat, w_op, scan_c, ctx_state)
        y_lat = _ssm_y(su_lat, hf_lat, hb_lat, m_op, v_op)
        d_skip = ssm_d[layer].reshape(1, -1)
        wg = w_glu[layer].astype(BF16)
        bg = b_glu[layer].reshape(1, -1)
        mix_s = _ssm_out(y_lat, p_lat, d_skip, wg, bg, bm)

        wo = w_out[layer].astype(BF16)
        slabs = (wo[:attn_w], wo[attn_w:2 * attn_w], wo[2 * attn_w:])
        xs_new = _outproj((mix_a, mix_b, mix_s), slabs, xs, gate[0:1],
                          final_gain.reshape(1, d) if last else None, bm)
        if not last:
            sinks = jnp.stack([jnp.full_like(b_sink[layer], NEG_INF), b_sink[layer]], axis=0)
            mix_ab_c = _attn_ctx(p_ctx, sinks)
            y_ctx = _ssm_y(su_ctx, hf_ctx, hb_ctx, m_op, v_op)
            mix_s_c = _ssm_out(y_ctx, p_ctx, d_skip, wg, bg, n_ctx)
            cs = _outproj((mix_ab_c, mix_s_c), (wo[:2 * attn_w], wo[2 * attn_w:]), cs, gate[1:2], None, n_ctx)
        xs = xs_new
    return xs[None]
```

```python
import functools
import math

import jax
import jax.numpy as jnp
import numpy as np
from jax import lax
from jax.experimental import pallas as pl
from jax.experimental.pallas import tpu as pltpu

F32 = jnp.float32
BF16 = jnp.bfloat16

HEAD_DIM = 128
GRID_W = 64
Q_PER_KV = 3
KV_HEADS = 2
Q_HEADS = Q_PER_KV * KV_HEADS
WINDOW = 128
SSM_GROUP = 16
SSM_STATE = 64
SSM_T = 8
SSM_BLOCK_GROUPS = HEAD_DIM // SSM_GROUP
SSM_BLOCK_STATES = SSM_BLOCK_GROUPS * SSM_STATE
SCAN_ROWS = 8
SSM_ROW_BLOCK = 256
ONES_ROWS = 16
ROPE_THETA = 10000.0
NORM_EPS = 1e-6
NEG_INF = -1e30
LOG2_E = math.log2(math.e)
VMEM_LIMIT_V7X = 56 * 1024 * 1024

SLOT_AQ, SLOT_BQ, SLOT_AK, SLOT_BK, SLOT_AV, SLOT_BV = 0, 6, 12, 14, 16, 18
SLOT_SU, SLOT_AG, SLOT_BG, SLOT_SG = 20, 24, 30, 36
N_SLOTS = 40
PANEL_SLOTS = 8


def _params(sem, vmem=VMEM_LIMIT_V7X):
    return pltpu.CompilerParams(dimension_semantics=sem, vmem_limit_bytes=vmem)


def _largest_block(total, limit):
    return max(b for b in range(HEAD_DIM, limit + 1, HEAD_DIM) if total % b == 0)


def _dot(a, b):
    return jnp.dot(a, b, preferred_element_type=F32)


def _dot_nt(a, b):
    return lax.dot_general(a, b, (((1,), (1,)), ((), ())), preferred_element_type=F32)


def _silu(x):
    return x * jax.nn.sigmoid(x)


def _mod_kernel(ct_ref, w_ref, b_ref, o_ref):
    s = _silu(ct_ref[...])
    w = w_ref[...]
    b = b_ref[...]
    o_ref[0:1, :] = jnp.sum(s[:, 0:1] * w, axis=0, keepdims=True) + b
    o_ref[1:2, :] = jnp.sum(s[:, 1:2] * w, axis=0, keepdims=True) + b


def _modulation(c_t, w_ada, b_ada):
    depth, d, n3 = w_ada.shape
    tn = 512
    return pl.pallas_call(
        _mod_kernel,
        out_shape=jax.ShapeDtypeStruct((depth, 2, n3), F32),
        grid=(depth, n3 // tn),
        in_specs=[
            pl.BlockSpec((d, 2), lambda l, j: (0, 0)),
            pl.BlockSpec((None, d, tn), lambda l, j: (l, 0, j)),
            pl.BlockSpec((None, 1, tn), lambda l, j: (l, 0, j)),
        ],
        out_specs=pl.BlockSpec((None, 2, tn), lambda l, j: (l, 0, j)),
        compiler_params=_params(("arbitrary", "arbitrary")),
        name="adaln_modulation",
    )(c_t, w_ada, b_ada.reshape(depth, 1, n3))


def _slot_config(slot):
    scale = HEAD_DIM ** -0.5 * LOG2_E
    if slot < SLOT_BQ:
        return 0, True, scale, False
    if slot < SLOT_AK:
        return None, True, scale, False
    if slot < SLOT_BK:
        return 1, True, None, False
    if slot < SLOT_AV:
        return None, True, None, False
    if slot < SLOT_AG:
        return None, False, None, False
    return None, False, None, True


def _inproj_kernel(*refs, rope):
    if rope:
        x_ref, shift_ref, scale_ref, gain_ref, w_ref, qk_ref, cos_ref, sin_ref, o_ref, qt_ref, vt_ref, su_ref = refs
    else:
        x_ref, shift_ref, scale_ref, gain_ref, w_ref, qk_ref, o_ref, qt_ref, vt_ref, su_ref = refs
    x = x_ref[...]
    y = x * lax.rsqrt(jnp.mean(x * x, axis=-1, keepdims=True) + NORM_EPS) * gain_ref[...]
    h = (y * (1.0 + scale_ref[...]) + shift_ref[...]).astype(BF16)
    lane = lax.broadcasted_iota(jnp.int32, (1, HEAD_DIM), 1)
    first_half = (lane & 63) < 32
    bn = PANEL_SLOTS * HEAD_DIM
    vt_rows = HEAD_DIM + ONES_ROWS

    for panel in range(N_SLOTS // PANEL_SLOTS):
        acc = _dot(h, w_ref[:, panel * bn:(panel + 1) * bn])
        for k in range(PANEL_SLOTS):
            slot = panel * PANEL_SLOTS + k
            gain_row, rotary, scale, act = _slot_config(slot)
            t = acc[:, k * HEAD_DIM:(k + 1) * HEAD_DIM]
            if gain_row is not None:
                t = (t * lax.rsqrt(jnp.mean(t * t, axis=-1, keepdims=True) + NORM_EPS)
                     * qk_ref[gain_row:gain_row + 1, :])
            if rotary and rope:
                partner = jnp.where(first_half, pltpu.roll(t, 96, 1), pltpu.roll(t, 32, 1))
                t = t * cos_ref[...] + partner * sin_ref[...]
            if scale is not None:
                t = t * scale
            if act:
                t = _silu(t)
            o_ref[:, slot * HEAD_DIM:(slot + 1) * HEAD_DIM] = t.astype(BF16)
            if SLOT_AQ <= slot < SLOT_BQ:
                qt_ref[(slot - SLOT_AQ) * HEAD_DIM:(slot - SLOT_AQ + 1) * HEAD_DIM, :] = t.T.astype(BF16)
            if SLOT_AV <= slot < SLOT_BV:
                base = (slot - SLOT_AV) * vt_rows
                vt_ref[base:base + HEAD_DIM, :] = t.T.astype(BF16)
                vt_ref[base + HEAD_DIM:base + vt_rows, :] = jnp.ones((ONES_ROWS, t.shape[0]), BF16)
            if SLOT_SU <= slot < SLOT_AG:
                su_ref[:, (slot - SLOT_SU) * HEAD_DIM:(slot - SLOT_SU + 1) * HEAD_DIM] = t


def _inproj(x, shift, scale, gain, w, qk_gain, rope_tables, bm):
    n, d = x.shape
    width = w.shape[1]
    rope = rope_tables is not None
    row = lambda i: (i, 0)
    col = lambda i: (0, i)
    const = lambda i: (0, 0)
    in_specs = [
        pl.BlockSpec((bm, d), row),
        pl.BlockSpec((1, d), const),
        pl.BlockSpec((1, d), const),
        pl.BlockSpec((1, d), const),
        pl.BlockSpec((d, width), const, pipeline_mode=pl.Buffered(1)),
        pl.BlockSpec((2, HEAD_DIM), const),
    ]
    args = [x, shift, scale, gain, w, qk_gain]
    if rope:
        in_specs += [pl.BlockSpec((bm, HEAD_DIM), row), pl.BlockSpec((bm, HEAD_DIM), row)]
        args += list(rope_tables)
    qt_rows = Q_HEADS * HEAD_DIM
    vt_rows = KV_HEADS * (HEAD_DIM + ONES_ROWS)
    su_width = (SLOT_AG - SLOT_SU) * HEAD_DIM
    return pl.pallas_call(
        functools.partial(_inproj_kernel, rope=rope),
        out_shape=(
            jax.ShapeDtypeStruct((n, width), BF16),
            jax.ShapeDtypeStruct((qt_rows, n), BF16),
            jax.ShapeDtypeStruct((vt_rows, n), BF16),
            jax.ShapeDtypeStruct((n, su_width), F32),
        ),
        grid=(n // bm,),
        in_specs=in_specs,
        out_specs=(
            pl.BlockSpec((bm, width), row),
            pl.BlockSpec((qt_rows, bm), col),
            pl.BlockSpec((vt_rows, bm), col),
            pl.BlockSpec((bm, su_width), row),
        ),
        compiler_params=_params(("arbitrary",)),
        name="inproj_rope" if rope else "inproj_ctx",
    )(*args)


def _stack_heads(ref, rows=None):
    sl = slice(None) if rows is None else rows
    return jnp.concatenate([ref[sl, g * HEAD_DIM:(g + 1) * HEAD_DIM] for g in range(Q_PER_KV)], axis=0)


def _attn_global_kernel(qt_ref, qn_ref, k0_ref, kn_ref, vt_ref, g_ref, o_ref, qa_s, s_s, mb_s, m_s, acc_s,
                        *, bq, nk, tq):
    qi = pl.program_id(1)
    ki = pl.program_id(2)
    tiles = [(g, c) for g in range(Q_PER_KV) for c in range(bq // tq)]

    def lanes(g, c):
        return slice(g * bq + c * tq, g * bq + (c + 1) * tq)

    def scores(k_ref, g, c):
        s = _dot(k_ref[...], qa_s[g * HEAD_DIM:(g + 1) * HEAD_DIM, c * tq:(c + 1) * tq])
        s_s[:, lanes(g, c)] = s
        mb_s[:, lanes(g, c)] = jnp.max(s, axis=0, keepdims=True)

    @pl.when((qi == 0) & (ki == 0))
    def _():
        qa_s[...] = qt_ref[...]
        for g, c in tiles:
            scores(k0_ref, g, c)

    @pl.when(ki == 0)
    def _():
        m_s[...] = jnp.full(m_s.shape, NEG_INF, F32)
        acc_s[...] = jnp.zeros(acc_s.shape, F32)

    @pl.when(ki == nk - 1)
    def _():
        qa_s[...] = qn_ref[...]

    vt = vt_ref[...]
    for g, c in tiles:
        cols = lanes(g, c)
        m_prev = m_s[:, cols]
        m_new = jnp.maximum(m_prev, mb_s[:, cols])
        alpha = jnp.exp2(m_prev - m_new)
        p = jnp.exp2((s_s[:, cols] - m_new).astype(BF16))
        acc_s[:, cols] = alpha * acc_s[:, cols] + _dot(vt, p)
        m_s[:, cols] = m_new
        scores(kn_ref, g, c)

    @pl.when(ki == nk - 1)
    def _():
        for g in range(Q_PER_KV):
            cols = slice(g * bq, (g + 1) * bq)
            out_t = acc_s[0:HEAD_DIM, cols] / acc_s[HEAD_DIM:HEAD_DIM + 1, cols]
            gate = g_ref[:, g * HEAD_DIM:(g + 1) * HEAD_DIM].astype(F32)
            o_ref[:, g * HEAD_DIM:(g + 1) * HEAD_DIM] = (out_t.T * gate).astype(BF16)


def _attn_global(p_lat, p_ctx, qt, vt_lat, vt_ctx, bq, bk):
    n = p_lat.shape[0]
    hd = HEAD_DIM
    qw = Q_PER_KV * hd
    ones_rows = ONES_ROWS
    k_all = jnp.concatenate([p_lat[:, SLOT_AK * hd:SLOT_BK * hd], p_ctx[:, SLOT_AK * hd:SLOT_BK * hd]], axis=0)
    vt_ext = jnp.concatenate([vt_lat, vt_ctx], axis=1)
    n_kv = k_all.shape[0]
    assert n_kv % bk == 0 and n % bq == 0
    nk = n_kv // bk
    nq = n // bq
    width = Q_PER_KV * bq
    return pl.pallas_call(
        functools.partial(_attn_global_kernel, bq=bq, nk=nk, tq=512),
        out_shape=jax.ShapeDtypeStruct((n, Q_HEADS * hd), BF16),
        grid=(KV_HEADS, nq, nk),
        in_specs=[
            pl.BlockSpec((qw, bq), lambda h, i, k: (h, i)),
            pl.BlockSpec((qw, bq), lambda h, i, k: (h, jnp.minimum(i + 1, nq - 1))),
            pl.BlockSpec((bk, hd), lambda h, i, k: (0, h)),
            pl.BlockSpec((bk, hd), lambda h, i, k: ((k + 1) % nk, h)),
            pl.BlockSpec((hd + ones_rows, bk), lambda h, i, k: (h, k)),
            pl.BlockSpec((bq, qw), lambda h, i, k: (i, SLOT_AG // Q_PER_KV + h)),
        ],
        out_specs=pl.BlockSpec((bq, qw), lambda h, i, k: (i, h)),
        scratch_shapes=[
            pltpu.VMEM((qw, bq), BF16),
            pltpu.VMEM((bk, width), F32),
            pltpu.VMEM((1, width), F32),
            pltpu.VMEM((1, width), F32),
            pltpu.VMEM((hd + ones_rows, width), F32),
        ],
        compiler_params=_params(("arbitrary", "arbitrary", "arbitrary")),
        name="attn_global",
    )(qt, qt, k_all, k_all, vt_ext, p_lat)


def _sink_column(sink_ref, base, rows_per_head, t=None):
    row = lax.broadcasted_iota(jnp.int32, (Q_PER_KV * rows_per_head, 1), 0)
    get = (lambda g: sink_ref[base + g]) if t is None else (lambda g: sink_ref[t, base + g])
    get = functools.partial(lambda f, g: f(g) * LOG2_E, get)
    return jnp.where(row < rows_per_head, get(0), jnp.where(row < 2 * rows_per_head, get(1), get(2)))


def _attn_window_kernel(sink_ref, q_ref, kp_ref, km_ref, kn_ref, vp_ref, vm_ref, vn_ref, kc_ref, vc_ref,
                        g_ref, o_ref, *, bq, n):
    h = pl.program_id(0)
    qi = pl.program_id(1)
    kcat = jnp.concatenate([kp_ref[...], km_ref[...], kn_ref[...]], axis=0)
    vcat = jnp.concatenate([vp_ref[...], vm_ref[...], vn_ref[...]], axis=0)
    kc = kc_ref[...]
    vc = vc_ref[...]
    span = 3 * WINDOW
    rows = Q_PER_KV * WINDOW
    r = lax.broadcasted_iota(jnp.int32, (rows, span), 0) & (WINDOW - 1)
    cidx = lax.broadcasted_iota(jnp.int32, (rows, span), 1)
    rel = cidx - r
    band = (rel >= 0) & (rel <= 2 * WINDOW)
    sink = _sink_column(sink_ref, h * Q_PER_KV, WINDOW)
    n_sub = bq // WINDOW
    q3s = [_stack_heads(q_ref, slice(sb * WINDOW, (sb + 1) * WINDOW)) for sb in range(n_sub)]
    band_scores = [_dot_nt(q3s[sb], kcat[sb * WINDOW:sb * WINDOW + span]) for sb in range(n_sub)]
    ctx_scores = [_dot_nt(q3s[sb], kc) for sb in range(n_sub)]
    for sb in range(n_sub):
        vw = vcat[sb * WINDOW:sb * WINDOW + span]
        key_pos = qi * bq + (sb - 1) * WINDOW + cidx
        valid = band & (key_pos >= 0) & (key_pos < n)
        s = jnp.where(valid, band_scores[sb], NEG_INF)
        sc = ctx_scores[sb]
        m = jnp.maximum(jnp.maximum(jnp.max(s, axis=-1, keepdims=True), jnp.max(sc, axis=-1, keepdims=True)), sink)
        p = jnp.exp2(s - m)
        pc = jnp.exp2(sc - m)
        den = jnp.sum(p, axis=-1, keepdims=True) + jnp.sum(pc, axis=-1, keepdims=True) + jnp.exp2(sink - m)
        out = (_dot(p.astype(BF16), vw) + _dot(pc.astype(BF16), vc)) / den
        for g in range(Q_PER_KV):
            gate = g_ref[sb * WINDOW:(sb + 1) * WINDOW, g * HEAD_DIM:(g + 1) * HEAD_DIM].astype(F32)
            o_ref[sb * WINDOW:(sb + 1) * WINDOW, g * HEAD_DIM:(g + 1) * HEAD_DIM] = (
                out[g * WINDOW:(g + 1) * WINDOW] * gate).astype(BF16)


def _attn_window(p_lat, p_ctx, sink, bq):
    n = p_lat.shape[0]
    n_ctx = p_ctx.shape[0]
    qw = Q_PER_KV * HEAD_DIM
    per = bq // WINDOW
    last = n // WINDOW - 1
    prev_map = lambda slot: (lambda h, i: (jnp.maximum(i * per - 1, 0), slot + h))
    main_map = lambda slot: (lambda h, i: (i, slot + h))
    next_map = lambda slot: (lambda h, i: (jnp.minimum((i + 1) * per, last), slot + h))
    return pl.pallas_call(
        functools.partial(_attn_window_kernel, bq=bq, n=n),
        out_shape=jax.ShapeDtypeStruct((n, Q_HEADS * HEAD_DIM), BF16),
        grid=(KV_HEADS, n // bq),
        in_specs=[
            pl.BlockSpec(memory_space=pltpu.SMEM),
            pl.BlockSpec((bq, qw), lambda h, i: (i, SLOT_BQ // Q_PER_KV + h)),
            pl.BlockSpec((WINDOW, HEAD_DIM), prev_map(SLOT_BK)),
            pl.BlockSpec((bq, HEAD_DIM), main_map(SLOT_BK)),
            pl.BlockSpec((WINDOW, HEAD_DIM), next_map(SLOT_BK)),
            pl.BlockSpec((WINDOW, HEAD_DIM), prev_map(SLOT_BV)),
            pl.BlockSpec((bq, HEAD_DIM), main_map(SLOT_BV)),
            pl.BlockSpec((WINDOW, HEAD_DIM), next_map(SLOT_BV)),
            pl.BlockSpec((n_ctx, HEAD_DIM), lambda h, i: (0, SLOT_BK + h)),
            pl.BlockSpec((n_ctx, HEAD_DIM), lambda h, i: (0, SLOT_BV + h)),
            pl.BlockSpec((bq, qw), lambda h, i: (i, SLOT_BG // Q_PER_KV + h)),
        ],
        out_specs=pl.BlockSpec((bq, qw), lambda h, i: (i, h)),
        compiler_params=_params(("arbitrary", "arbitrary")),
        name="attn_window",
    )(sink, p_lat, p_lat, p_lat, p_lat, p_lat, p_lat, p_lat, p_ctx, p_ctx, p_lat)


def _attn_ctx_kernel(sink_ref, q_ref, k_ref, v_ref, g_ref, o_ref, *, n_ctx):
    idx = pl.program_id(0)
    t = idx // KV_HEADS
    h = idx % KV_HEADS
    q3 = _stack_heads(q_ref)
    s = _dot_nt(q3, k_ref[...])
    sink = _sink_column(sink_ref, h * Q_PER_KV, n_ctx, t=t)
    m = jnp.maximum(jnp.max(s, axis=-1, keepdims=True), sink)
    p = jnp.exp2(s - m)
    den = jnp.sum(p, axis=-1, keepdims=True) + jnp.exp2(sink - m)
    out = _dot(p.astype(BF16), v_ref[...]) / den
    for g in range(Q_PER_KV):
        gate = g_ref[:, g * HEAD_DIM:(g + 1) * HEAD_DIM].astype(F32)
        o_ref[:, g * HEAD_DIM:(g + 1) * HEAD_DIM] = (out[g * n_ctx:(g + 1) * n_ctx] * gate).astype(BF16)


def _attn_ctx(p_ctx, sinks):
    n_ctx = p_ctx.shape[0]
    qw = Q_PER_KV * HEAD_DIM
    return pl.pallas_call(
        functools.partial(_attn_ctx_kernel, n_ctx=n_ctx),
        out_shape=jax.ShapeDtypeStruct((n_ctx, 2 * Q_HEADS * HEAD_DIM), BF16),
        grid=(2 * KV_HEADS,),
        in_specs=[
            pl.BlockSpec(memory_space=pltpu.SMEM),
            pl.BlockSpec((n_ctx, qw), lambda i: (0, i)),
            pl.BlockSpec((n_ctx, HEAD_DIM), lambda i: (0, SLOT_AK + i)),
            pl.BlockSpec((n_ctx, HEAD_DIM), lambda i: (0, SLOT_AV + i)),
            pl.BlockSpec((n_ctx, qw), lambda i: (0, SLOT_AG // Q_PER_KV + i)),
        ],
        out_specs=pl.BlockSpec((n_ctx, qw), lambda i: (0, i)),
        compiler_params=_params(("arbitrary",)),
        name="attn_ctx",
    )(sinks, p_ctx, p_ctx, p_ctx, p_ctx)


def _complex_mul(a_re, a_im, b_re, b_im):
    return a_re * b_re - a_im * b_im, a_re * b_im + a_im * b_re


def _dot_split(a, b):
    a_hi = a.astype(BF16)
    b_hi = b.astype(BF16)
    a_lo = (a - a_hi.astype(F32)).astype(BF16)
    b_lo = (b - b_hi.astype(F32)).astype(BF16)
    return _dot(a_hi, b_hi) + _dot(a_hi, b_lo) + _dot(a_lo, b_hi)


def _ssm_prep_kernel(lam_r_ref, bd_b_ref, bd_c_ref, m_ref, w_ref, v_ref, sc_ref):
    T = SSM_T
    row8 = lax.broadcasted_iota(jnp.int32, (SCAN_ROWS, 1), 0)
    n_exp = T + 1 + 3 + SCAN_ROWS
    n_col = -(-(T + 1) // SCAN_ROWS) * SCAN_ROWS
    erow = lax.broadcasted_iota(jnp.int32, (n_exp, 1), 0)
    lag_kernels = []
    for d in range(2):
        lam_re, lam_im = lam_r_ref[d, 0:1, :], lam_r_ref[d, 1:2, :]
        step = jnp.exp(lam_r_ref[d, 2:3, :])
        dist = erow - (T + 4)
        dist = dist if d == 0 else SCAN_ROWS - 1 - dist
        expo = jnp.where(erow <= T, erow,
                         jnp.where(erow == T + 1, 2 * T,
                                   jnp.where(erow == T + 2, 4 * T,
                                             jnp.where(erow == T + 3, 8 * T, T * dist)))).astype(F32)
        mag = jnp.exp(lam_re * step * expo)
        ang = lam_im * step * expo
        p_re, p_im = mag * jnp.cos(ang), mag * jnp.sin(ang)
        pc_re, pc_im = p_re[0:n_col].T, p_im[0:n_col].T

        nr, ni = p_re[1:2] - 1.0, p_im[1:2]
        den = lam_re * lam_re + lam_im * lam_im
        q_re = (nr * lam_re + ni * lam_im) / den
        q_im = (ni * lam_re - nr * lam_im) / den
        bb_re, bb_im = _complex_mul(bd_b_ref[d, 0], bd_b_ref[d, 1], q_re, q_im)
        c_re, c_im = bd_c_ref[d, 0], bd_c_ref[d, 1]

        def w_rows(e):
            g_re, g_im = _complex_mul(bb_re, bb_im, p_re[e:e + 1], p_im[e:e + 1])
            return jnp.concatenate([g_re, g_im], axis=1)

        def v_cols(e):
            g_re, g_im = _complex_mul(c_re, c_im, pc_re[:, e:e + 1], pc_im[:, e:e + 1])
            return jnp.concatenate([g_re, -g_im], axis=0)

        w = jnp.concatenate([w_rows(T - 1 - t if d == 0 else t) for t in range(T)], axis=0)
        w_ref[d] = w.astype(BF16)
        v_ref[d] = jnp.concatenate([v_cols(t + 1 if d == 0 else T - t) for t in range(T)], axis=1).astype(BF16)
        lags = _dot_split(w, v_cols(0))
        lag_of_block = [(T - 1 - t if d == 0 else t) for t in range(T)]
        lag_kernels.append({lag: lags[t * HEAD_DIM:(t + 1) * HEAD_DIM] for t, lag in enumerate(lag_of_block)})

        ahead = (lambda k: row8 >= k) if d == 0 else (lambda k: row8 <= SCAN_ROWS - 1 - k)
        rows_of = {1: T, 2: T + 1, 4: T + 2, 8: T + 3}
        parts_re, parts_im = [], []
        for k in (1, 2, 4):
            r = rows_of[k]
            parts_re.append(jnp.where(ahead(k), p_re[r:r + 1], 0.0))
            parts_im.append(jnp.where(ahead(k), p_im[r:r + 1], 0.0))
        parts_re.append(p_re[T + 4:T + 4 + SCAN_ROWS])
        parts_im.append(p_im[T + 4:T + 4 + SCAN_ROWS])
        r = rows_of[8]
        parts_re.append(jnp.broadcast_to(p_re[r:r + 1], (SCAN_ROWS, p_re.shape[1])))
        parts_im.append(jnp.broadcast_to(p_im[r:r + 1], (SCAN_ROWS, p_im.shape[1])))
        sc_ref[d, 0] = jnp.concatenate(parts_re, axis=0)
        sc_ref[d, 1] = jnp.concatenate(parts_im, axis=0)

    fwd, bwd = lag_kernels
    for t in range(T):
        blocks = []
        for t2 in range(T):
            if t2 > t:
                blocks.append(fwd[t2 - t])
            elif t2 < t:
                blocks.append(bwd[t - t2])
            else:
                blocks.append(fwd[0] + bwd[0])
        m_ref[t * HEAD_DIM:(t + 1) * HEAD_DIM, :] = jnp.concatenate(blocks, axis=1).astype(BF16)


def _ssm_prep(lam_re, lam_im, log_step, b_re, b_im, c_re, c_im):
    n_dir, groups, state = lam_re.shape
    bg = SSM_BLOCK_GROUPS
    nb = groups // bg
    ns = SSM_BLOCK_STATES
    width = SSM_T * HEAD_DIM
    step = jnp.broadcast_to(log_step[:, :, None], lam_re.shape)
    rows = jnp.stack([lam_re, lam_im, step], axis=1)
    lam_r = rows.reshape(n_dir, 3, nb, ns).transpose(2, 0, 1, 3)
    eye = jnp.eye(bg, dtype=F32)

    def b_layout(b):
        bt = b.reshape(n_dir, nb, bg, state, SSM_GROUP)
        return jnp.einsum('dbgpj,gh->bdgjhp', bt, eye).reshape(nb, n_dir, HEAD_DIM, ns)

    def c_layout(c):
        ct = c.reshape(n_dir, nb, bg, SSM_GROUP, state)
        return jnp.einsum('dbhip,gh->bdhpgi', ct, eye).reshape(nb, n_dir, ns, HEAD_DIM)

    bd_b = jnp.stack([b_layout(b_re), b_layout(b_im)], axis=2)
    bd_c = jnp.stack([c_layout(c_re), c_layout(c_im)], axis=2)
    return pl.pallas_call(
        _ssm_prep_kernel,
        out_shape=(
            jax.ShapeDtypeStruct((nb, width, width), BF16),
            jax.ShapeDtypeStruct((nb, n_dir, width, 2 * ns), BF16),
            jax.ShapeDtypeStruct((nb, n_dir, 2 * ns, width), BF16),
            jax.ShapeDtypeStruct((nb, n_dir, 2, 5 * SCAN_ROWS, ns), F32),
        ),
        grid=(nb,),
        in_specs=[
            pl.BlockSpec((None, n_dir, 3, ns), lambda b: (b, 0, 0, 0)),
            pl.BlockSpec((None, n_dir, 2, HEAD_DIM, ns), lambda b: (b, 0, 0, 0, 0)),
            pl.BlockSpec((None, n_dir, 2, ns, HEAD_DIM), lambda b: (b, 0, 0, 0, 0)),
        ],
        out_specs=(
            pl.BlockSpec((None, width, width), lambda b: (b, 0, 0)),
            pl.BlockSpec((None, n_dir, width, 2 * ns), lambda b: (b, 0, 0, 0)),
            pl.BlockSpec((None, n_dir, 2 * ns, width), lambda b: (b, 0, 0, 0)),
            pl.BlockSpec((None, n_dir, 2, 5 * SCAN_ROWS, ns), lambda b: (b, 0, 0, 0, 0)),
        ),
        compiler_params=_params(("arbitrary",)),
        name="ssm_prep",
    )(lam_r, bd_b, bd_c)


def _chunk_rows(u_ref, rb):
    return jnp.concatenate([u_ref[pl.ds(t, rb, stride=SSM_T), :] for t in range(SSM_T)], axis=1).astype(BF16)


def _ssm_states_kernel(uf_ref, ub_ref, w_ref, sc_ref, cin_ref, hf_ref, hb_ref, cout_ref, s_s, carry_s, *, rb, nrb):
    step = pl.program_id(1)
    ns = SSM_BLOCK_STATES
    row8 = lax.broadcasted_iota(jnp.int32, (SCAN_ROWS, 1), 0)

    @pl.when(step == 0)
    def _():
        carry_s[...] = cin_ref[...]

    for d, (u_ref, h_ref) in enumerate(((uf_ref, hf_ref), (ub_ref, hb_ref))):
        s_s[d] = _dot(_chunk_rows(u_ref, rb), w_ref[d])
        inner = (row8 >= 1) if d == 0 else (row8 <= SCAN_ROWS - 2)
        em = jnp.where(inner, 1.0, 0.0).astype(F32)
        consts = [(sc_ref[d, 0, i * SCAN_ROWS:(i + 1) * SCAN_ROWS, :], sc_ref[d, 1, i * SCAN_ROWS:(i + 1) * SCAN_ROWS, :])
                  for i in range(5)]
        n_groups = rb // SCAN_ROWS

        def body(g, carry, d=d, em=em, consts=consts):
            c_re, c_im = carry
            gi = g if d == 0 else n_groups - 1 - g
            rows = pl.ds(pl.multiple_of(gi * SCAN_ROWS, SCAN_ROWS), SCAN_ROWS)
            p_re = s_s[d, rows, 0:ns]
            p_im = s_s[d, rows, ns:2 * ns]
            for i, k in enumerate((1, 2, 4)):
                shift = k if d == 0 else SCAN_ROWS - k
                m_re, m_im = _complex_mul(consts[i][0], consts[i][1], pltpu.roll(p_re, shift, 0), pltpu.roll(p_im, shift, 0))
                p_re, p_im = p_re + m_re, p_im + m_im
            shift = 1 if d == 0 else SCAN_ROWS - 1
            k_re, k_im = _complex_mul(consts[3][0], consts[3][1], c_re, c_im)
            s_s[d, rows, 0:ns] = em * pltpu.roll(p_re, shift, 0) + k_re
            s_s[d, rows, ns:2 * ns] = em * pltpu.roll(p_im, shift, 0) + k_im
            last = SCAN_ROWS - 1 if d == 0 else 0
            n_re, n_im = _complex_mul(consts[4][0][0:1], consts[4][1][0:1], c_re, c_im)
            return p_re[last:last + 1] + n_re, p_im[last:last + 1] + n_im

        c_re, c_im = lax.fori_loop(0, n_groups, body, (carry_s[d, 0, 0:1, :], carry_s[d, 1, 0:1, :]), unroll=True)
        carry_s[d, 0, 0:1, :] = c_re
        carry_s[d, 1, 0:1, :] = c_im
        h_ref[...] = s_s[d].astype(BF16)

    @pl.when(step == nrb - 1)
    def _():
        cout_ref[...] = carry_s[...]


def _ssm_states(u, w, sc, carry_in):
    n_tok, width = u.shape
    nb = width // HEAD_DIM
    n_chunks = n_tok // SSM_T
    rb = min(SSM_ROW_BLOCK, n_chunks)
    nrb = n_chunks // rb
    ns = SSM_BLOCK_STATES
    h_shape = jax.ShapeDtypeStruct((n_chunks, nb * 2 * ns), BF16)
    return pl.pallas_call(
        functools.partial(_ssm_states_kernel, rb=rb, nrb=nrb),
        out_shape=(h_shape, h_shape, jax.ShapeDtypeStruct(carry_in.shape, F32)),
        grid=(nb, nrb),
        in_specs=[
            pl.BlockSpec((rb * SSM_T, HEAD_DIM), lambda b, r: (r, b)),
            pl.BlockSpec((rb * SSM_T, HEAD_DIM), lambda b, r: (nrb - 1 - r, b)),
            pl.BlockSpec((None, 2, SSM_T * HEAD_DIM, 2 * ns), lambda b, r: (b, 0, 0, 0)),
            pl.BlockSpec((None, 2, 2, 5 * SCAN_ROWS, ns), lambda b, r: (b, 0, 0, 0, 0)),
            pl.BlockSpec((None, 2, 2, SCAN_ROWS, ns), lambda b, r: (b, 0, 0, 0, 0)),
        ],
        out_specs=(
            pl.BlockSpec((rb, 2 * ns), lambda b, r: (r, b)),
            pl.BlockSpec((rb, 2 * ns), lambda b, r: (nrb - 1 - r, b)),
            pl.BlockSpec((None, 2, 2, SCAN_ROWS, ns), lambda b, r: (b, 0, 0, 0, 0)),
        ),
        scratch_shapes=[pltpu.VMEM((2, rb, 2 * ns), F32), pltpu.VMEM((2, 2, SCAN_ROWS, ns), F32)],
        compiler_params=_params(("arbitrary", "arbitrary")),
        name="ssm_states",
    )(u, u, w, sc, carry_in)


def _ssm_y_kernel(u_ref, hf_ref, hb_ref, m_ref, v_ref, y_ref, *, rb):
    ycat = (_dot(_chunk_rows(u_ref, rb), m_ref[...]) + _dot(hf_ref[...], v_ref[0]) + _dot(hb_ref[...], v_ref[1]))
    for t in range(SSM_T):
        y_ref[pl.ds(t, rb, stride=SSM_T), :] = ycat[:, t * HEAD_DIM:(t + 1) * HEAD_DIM]


def _ssm_y(u, hf, hb, m, v):
    n_tok, width = u.shape
    nb = width // HEAD_DIM
    n_chunks = n_tok // SSM_T
    rb = min(SSM_ROW_BLOCK, n_chunks)
    ns = SSM_BLOCK_STATES
    cw = SSM_T * HEAD_DIM
    return pl.pallas_call(
        functools.partial(_ssm_y_kernel, rb=rb),
        out_shape=jax.ShapeDtypeStruct((n_tok, width), F32),
        grid=(nb, n_chunks // rb),
        in_specs=[
            pl.BlockSpec((rb * SSM_T, HEAD_DIM), lambda b, r: (r, b)),
            pl.BlockSpec((rb, 2 * ns), lambda b, r: (r, b)),
            pl.BlockSpec((rb, 2 * ns), lambda b, r: (r, b)),
            pl.BlockSpec((None, cw, cw), lambda b, r: (b, 0, 0)),
            pl.BlockSpec((None, 2, 2 * ns, cw), lambda b, r: (b, 0, 0, 0)),
        ],
        out_specs=pl.BlockSpec((rb * SSM_T, HEAD_DIM), lambda b, r: (r, b)),
        compiler_params=_params(("arbitrary", "arbitrary")),
        name="ssm_y",
    )(u, hf, hb, m, v)


def _outproj_kernel(*refs, n_attn, final):
    mix = refs[:n_attn]
    y_ref, u_ref, sg_ref, d_ref, wg_ref, bg_ref, w_ref, x_ref, gate_ref = refs[n_attn:n_attn + 9]
    rest = refs[n_attn + 9:]
    y = y_ref[...] + d_ref[...] * u_ref[...].astype(F32)
    y = jax.nn.gelu(y, approximate=True)
    z = _dot(y.astype(BF16), wg_ref[...]) + bg_ref[...]
    mix_s = (y * jax.nn.sigmoid(z) * sg_ref[...].astype(F32)).astype(BF16)
    acc = None
    start = 0
    for a in [m[...] for m in mix] + [mix_s]:
        part = _dot(a, w_ref[start:start + a.shape[1], :])
        acc = part if acc is None else acc + part
        start += a.shape[1]
    x = x_ref[...] + gate_ref[...] * acc
    if final:
        fg_ref, o_ref = rest
        x = x * lax.rsqrt(jnp.mean(x * x, axis=-1, keepdims=True) + NORM_EPS) * fg_ref[...]
    else:
        (o_ref,) = rest
    o_ref[...] = x


def _outproj(attn_mixes, y, p, d_skip, w_glu, b_glu, w_out, x, gate, final_gain, bm):
    n, d = x.shape
    s_width = y.shape[1]
    blk = s_width // HEAD_DIM
    row = lambda i: (i, 0)
    const = lambda i: (0, 0)
    in_specs = [pl.BlockSpec((bm, a.shape[1]), row) for a in attn_mixes]
    in_specs += [
        pl.BlockSpec((bm, s_width), row),
        pl.BlockSpec((bm, s_width), lambda i: (i, SLOT_SU // blk)),
        pl.BlockSpec((bm, s_width), lambda i: (i, SLOT_SG // blk)),
        pl.BlockSpec((1, s_width), const),
        pl.BlockSpec((s_width, s_width), const),
        pl.BlockSpec((1, s_width), const),
        pl.BlockSpec(w_out.shape, const, pipeline_mode=pl.Buffered(1)),
        pl.BlockSpec((bm, d), row),
        pl.BlockSpec((1, d), const),
    ]
    args = list(attn_mixes) + [y, p, p, d_skip, w_glu, b_glu, w_out, x, gate]
    final = final_gain is not None
    if final:
        in_specs.append(pl.BlockSpec((1, d), const))
        args.append(final_gain)
    return pl.pallas_call(
        functools.partial(_outproj_kernel, n_attn=len(attn_mixes), final=final),
        out_shape=jax.ShapeDtypeStruct((n, d), F32),
        grid=(n // bm,),
        in_specs=in_specs,
        out_specs=pl.BlockSpec((bm, d), row),
        compiler_params=_params(("arbitrary",)),
        name="outproj_final" if final else "outproj",
    )(*args)


def _rope_tables(n):
    pos = np.arange(n)
    half = HEAD_DIM // 4
    freqs = (ROPE_THETA ** (-np.arange(half, dtype=np.float32) / half)).astype(np.float32)
    ang_r = ((pos // GRID_W).astype(np.float32)[:, None] * freqs[None, :]).astype(np.float64)
    ang_c = ((pos % GRID_W).astype(np.float32)[:, None] * freqs[None, :]).astype(np.float64)
    cos = np.concatenate([np.cos(ang_r)] * 2 + [np.cos(ang_c)] * 2, axis=-1)
    sin = np.concatenate([-np.sin(ang_r), np.sin(ang_r), -np.sin(ang_c), np.sin(ang_c)], axis=-1)
    return jnp.asarray(cos, F32), jnp.asarray(sin, F32)


def _reorder_w_in(w):
    hd = HEAD_DIM
    sizes = (6 * hd, 2 * hd, 2 * hd, 6 * hd, 6 * hd, 2 * hd, 2 * hd, 6 * hd, 4 * hd, 4 * hd)
    names = ("aq", "ak", "av", "ag", "bq", "bk", "bv", "bg", "su", "sg")
    parts, start = {}, 0
    for name, size in zip(names, sizes):
        parts[name] = w[:, start:start + size]
        start += size
    order = ("aq", "bq", "ak", "bk", "av", "bv", "su", "ag", "bg", "sg")
    return jnp.concatenate([parts[k] for k in order], axis=1).astype(BF16)


def kernel(x, c, ctx, c_ctx, w_ada, b_ada, norm_gain, w_in, a_q_gain, a_k_gain, b_sink, ssm_lambda_re,
           ssm_lambda_im, ssm_log_step, ssm_b_re, ssm_b_im, ssm_c_re, ssm_c_im, ssm_d, w_glu, b_glu, w_out,
           final_gain):
    depth, d = norm_gain.shape
    n = x.shape[1]
    n_ctx = ctx.shape[1]
    assert x.shape[0] == 1 and n % 512 == 0 and n_ctx % (SSM_T * SCAN_ROWS) == 0
    xs = x[0]
    cs = ctx[0]
    bm = 512

    c_t = jnp.concatenate([c.reshape(d, 1), c_ctx.reshape(d, 1)], axis=1)
    mod = _modulation(c_t, w_ada, b_ada)
    tables = _rope_tables(n)

    for layer in range(depth):
        last = layer == depth - 1
        shift, scale, gate = (mod[layer, :, i * d:(i + 1) * d] for i in range(3))
        w = _reorder_w_in(w_in[layer])
        gain = norm_gain[layer].reshape(1, d)
        qk_gain = jnp.stack([a_q_gain[layer], a_k_gain[layer]], axis=0)
        p_lat, qt_lat, vt_lat, su_lat = _inproj(xs, shift[0:1], scale[0:1], gain, w, qk_gain, tables, bm)
        p_ctx, _, vt_ctx, su_ctx = _inproj(cs, shift[1:2], scale[1:2], gain, w, qk_gain, None, n_ctx)

        mix_a = _attn_global(p_lat, p_ctx, qt_lat, vt_lat, vt_ctx, bq=_largest_block(n, 1024),
                             bk=_largest_block(n + n_ctx, 1280))
        mix_b = _attn_window(p_lat, p_ctx, b_sink[layer], bq=_largest_block(n, 1024))

        m_op, w_op, v_op, scan_c = _ssm_prep(ssm_lambda_re[layer], ssm_lambda_im[layer], ssm_log_step[layer],
                                            ssm_b_re[layer], ssm_b_im[layer], ssm_c_re[layer], ssm_c_im[layer])
        no_state = jnp.zeros((su_lat.shape[1] // HEAD_DIM, 2, 2, SCAN_ROWS, SSM_BLOCK_STATES), F32)
        hf_ctx, hb_ctx, ctx_state = _ssm_states(su_ctx, w_op, scan_c, no_state)
        hf_lat, hb_lat, _ = _ssm_states(su_lat, w_op, scan_c, ctx_state)
        y_lat = _ssm_y(su_lat, hf_lat, hb_lat, m_op, v_op)
        d_skip = ssm_d[layer].reshape(1, -1)
        wg = w_glu[layer].astype(BF16)
        bg = b_glu[layer].reshape(1, -1)
        wo = w_out[layer].astype(BF16)
        xs_new = _outproj((mix_a, mix_b), y_lat, p_lat, d_skip, wg, bg, wo, xs, gate[0:1],
                          final_gain.reshape(1, d) if last else None, bm)
        if not last:
            sinks = jnp.stack([jnp.full_like(b_sink[layer], NEG_INF), b_sink[layer]], axis=0)
            mix_ab_c = _attn_ctx(p_ctx, sinks)
            y_ctx = _ssm_y(su_ctx, hf_ctx, hb_ctx, m_op, v_op)
            cs = _outproj((mix_ab_c,), y_ctx, p_ctx, d_skip, wg, bg, wo, cs, gate[1:2], None, n_ctx)
        xs = xs_new
    return xs[None]
```

```python
import functools
import math

import jax
import jax.numpy as jnp
import numpy as np
from jax import lax
from jax.experimental import pallas as pl
from jax.experimental.pallas import tpu as pltpu

F32 = jnp.float32
BF16 = jnp.bfloat16

HEAD_DIM = 128
GRID_W = 64
Q_PER_KV = 3
KV_HEADS = 2
Q_HEADS = Q_PER_KV * KV_HEADS
WINDOW = 128
SSM_GROUP = 16
SSM_STATE = 64
SSM_T = 8
SSM_BLOCK_GROUPS = HEAD_DIM // SSM_GROUP
SSM_BLOCK_STATES = SSM_BLOCK_GROUPS * SSM_STATE
SCAN_ROWS = 8
SSM_ROW_BLOCK = 512
ONES_ROWS = 16
ROPE_THETA = 10000.0
NORM_EPS = 1e-6
NEG_INF = -1e30
LOG2_E = math.log2(math.e)
VMEM_LIMIT_V7X = 56 * 1024 * 1024

SLOT_AQ, SLOT_BQ, SLOT_AK, SLOT_BK, SLOT_AV, SLOT_BV = 0, 6, 12, 14, 16, 18
SLOT_SU, SLOT_AG, SLOT_BG, SLOT_SG = 20, 24, 30, 36
N_SLOTS = 40
PANEL_SLOTS = 8


def _params(sem, vmem=VMEM_LIMIT_V7X):
    return pltpu.CompilerParams(dimension_semantics=sem, vmem_limit_bytes=vmem)


def _largest_block(total, limit):
    return max(b for b in range(HEAD_DIM, limit + 1, HEAD_DIM) if total % b == 0)


def _dot(a, b):
    return jnp.dot(a, b, preferred_element_type=F32)


def _dot_nt(a, b):
    return lax.dot_general(a, b, (((1,), (1,)), ((), ())), preferred_element_type=F32)


def _silu(x):
    return x * jax.nn.sigmoid(x)


def _mod_kernel(ct_ref, w_ref, b_ref, o_ref):
    s = _silu(ct_ref[...])
    w = w_ref[...]
    b = b_ref[...]
    o_ref[0:1, :] = jnp.sum(s[:, 0:1] * w, axis=0, keepdims=True) + b
    o_ref[1:2, :] = jnp.sum(s[:, 1:2] * w, axis=0, keepdims=True) + b


def _modulation(c_t, w_ada, b_ada):
    depth, d, n3 = w_ada.shape
    tn = 512
    return pl.pallas_call(
        _mod_kernel,
        out_shape=jax.ShapeDtypeStruct((depth, 2, n3), F32),
        grid=(depth, n3 // tn),
        in_specs=[
            pl.BlockSpec((d, 2), lambda l, j: (0, 0)),
            pl.BlockSpec((None, d, tn), lambda l, j: (l, 0, j)),
            pl.BlockSpec((None, 1, tn), lambda l, j: (l, 0, j)),
        ],
        out_specs=pl.BlockSpec((None, 2, tn), lambda l, j: (l, 0, j)),
        compiler_params=_params(("arbitrary", "arbitrary")),
        name="adaln_modulation",
    )(c_t, w_ada, b_ada.reshape(depth, 1, n3))


def _slot_config(slot):
    scale = HEAD_DIM ** -0.5 * LOG2_E
    if slot < SLOT_BQ:
        return 0, True, scale, False
    if slot < SLOT_AK:
        return None, True, scale, False
    if slot < SLOT_BK:
        return 1, True, None, False
    if slot < SLOT_AV:
        return None, True, None, False
    if slot < SLOT_AG:
        return None, False, None, False
    return None, False, None, True


def _inproj_kernel(*refs, rope):
    if rope:
        x_ref, shift_ref, scale_ref, gain_ref, w_ref, qk_ref, cos_ref, sin_ref, o_ref, qt_ref, vt_ref, su_ref = refs
    else:
        x_ref, shift_ref, scale_ref, gain_ref, w_ref, qk_ref, o_ref, qt_ref, vt_ref, su_ref = refs
    x = x_ref[...]
    y = x * lax.rsqrt(jnp.mean(x * x, axis=-1, keepdims=True) + NORM_EPS) * gain_ref[...]
    h = (y * (1.0 + scale_ref[...]) + shift_ref[...]).astype(BF16)
    lane = lax.broadcasted_iota(jnp.int32, (1, HEAD_DIM), 1)
    first_half = (lane & 63) < 32
    bn = PANEL_SLOTS * HEAD_DIM
    vt_rows = HEAD_DIM + ONES_ROWS

    for panel in range(N_SLOTS // PANEL_SLOTS):
        acc = _dot(h, w_ref[:, panel * bn:(panel + 1) * bn])
        for k in range(PANEL_SLOTS):
            slot = panel * PANEL_SLOTS + k
            gain_row, rotary, scale, act = _slot_config(slot)
            t = acc[:, k * HEAD_DIM:(k + 1) * HEAD_DIM]
            if gain_row is not None:
                t = (t * lax.rsqrt(jnp.mean(t * t, axis=-1, keepdims=True) + NORM_EPS)
                     * qk_ref[gain_row:gain_row + 1, :])
            if rotary and rope:
                partner = jnp.where(first_half, pltpu.roll(t, 96, 1), pltpu.roll(t, 32, 1))
                t = t * cos_ref[...] + partner * sin_ref[...]
            if scale is not None:
                t = t * scale
            if act:
                t = _silu(t)
            o_ref[:, slot * HEAD_DIM:(slot + 1) * HEAD_DIM] = t.astype(BF16)
            if SLOT_AQ <= slot < SLOT_BQ:
                qt_ref[(slot - SLOT_AQ) * HEAD_DIM:(slot - SLOT_AQ + 1) * HEAD_DIM, :] = t.T.astype(BF16)
            if SLOT_AV <= slot < SLOT_BV:
                base = (slot - SLOT_AV) * vt_rows
                vt_ref[base:base + HEAD_DIM, :] = t.T.astype(BF16)
                vt_ref[base + HEAD_DIM:base + vt_rows, :] = jnp.ones((ONES_ROWS, t.shape[0]), BF16)
            if SLOT_SU <= slot < SLOT_AG:
                su_ref[:, (slot - SLOT_SU) * HEAD_DIM:(slot - SLOT_SU + 1) * HEAD_DIM] = t


def _inproj(x, shift, scale, gain, w, qk_gain, rope_tables, bm):
    n, d = x.shape
    width = w.shape[1]
    rope = rope_tables is not None
    row = lambda i: (i, 0)
    col = lambda i: (0, i)
    const = lambda i: (0, 0)
    in_specs = [
        pl.BlockSpec((bm, d), row),
        pl.BlockSpec((1, d), const),
        pl.BlockSpec((1, d), const),
        pl.BlockSpec((1, d), const),
        pl.BlockSpec((d, width), const, pipeline_mode=pl.Buffered(1)),
        pl.BlockSpec((2, HEAD_DIM), const),
    ]
    args = [x, shift, scale, gain, w, qk_gain]
    if rope:
        in_specs += [pl.BlockSpec((bm, HEAD_DIM), row), pl.BlockSpec((bm, HEAD_DIM), row)]
        args += list(rope_tables)
    qt_rows = Q_HEADS * HEAD_DIM
    vt_rows = KV_HEADS * (HEAD_DIM + ONES_ROWS)
    su_width = (SLOT_AG - SLOT_SU) * HEAD_DIM
    return pl.pallas_call(
        functools.partial(_inproj_kernel, rope=rope),
        out_shape=(
            jax.ShapeDtypeStruct((n, width), BF16),
            jax.ShapeDtypeStruct((qt_rows, n), BF16),
            jax.ShapeDtypeStruct((vt_rows, n), BF16),
            jax.ShapeDtypeStruct((n, su_width), F32),
        ),
        grid=(n // bm,),
        in_specs=in_specs,
        out_specs=(
            pl.BlockSpec((bm, width), row),
            pl.BlockSpec((qt_rows, bm), col),
            pl.BlockSpec((vt_rows, bm), col),
            pl.BlockSpec((bm, su_width), row),
        ),
        compiler_params=_params(("arbitrary",)),
        name="inproj_rope" if rope else "inproj_ctx",
    )(*args)


def _stack_heads(ref, rows=None):
    sl = slice(None) if rows is None else rows
    return jnp.concatenate([ref[sl, g * HEAD_DIM:(g + 1) * HEAD_DIM] for g in range(Q_PER_KV)], axis=0)


def _attn_global_kernel(qt_ref, qn_ref, k0_ref, kn_ref, vt_ref, g_ref, o_ref, qa_s, s_s, mb_s, m_s, acc_s,
                        *, bq, nk, tq):
    qi = pl.program_id(1)
    ki = pl.program_id(2)
    tiles = [(g, c) for g in range(Q_PER_KV) for c in range(bq // tq)]

    def lanes(g, c):
        return slice(g * bq + c * tq, g * bq + (c + 1) * tq)

    def scores(k_ref, g, c):
        s = _dot(k_ref[...], qa_s[g * HEAD_DIM:(g + 1) * HEAD_DIM, c * tq:(c + 1) * tq])
        s_s[:, lanes(g, c)] = s
        mb_s[:, lanes(g, c)] = jnp.max(s, axis=0, keepdims=True)

    @pl.when((qi == 0) & (ki == 0))
    def _():
        qa_s[...] = qt_ref[...]
        for g, c in tiles:
            scores(k0_ref, g, c)

    @pl.when(ki == 0)
    def _():
        m_s[...] = jnp.full(m_s.shape, NEG_INF, F32)
        acc_s[...] = jnp.zeros(acc_s.shape, F32)

    @pl.when(ki == nk - 1)
    def _():
        qa_s[...] = qn_ref[...]

    vt = vt_ref[...]
    for g, c in tiles:
        cols = lanes(g, c)
        m_prev = m_s[:, cols]
        m_new = jnp.maximum(m_prev, mb_s[:, cols])
        alpha = jnp.exp2(m_prev - m_new)
        p = jnp.exp2((s_s[:, cols] - m_new).astype(BF16))
        acc_s[:, cols] = alpha * acc_s[:, cols] + _dot(vt, p)
        m_s[:, cols] = m_new
        scores(kn_ref, g, c)

    @pl.when(ki == nk - 1)
    def _():
        for g in range(Q_PER_KV):
            cols = slice(g * bq, (g + 1) * bq)
            out_t = acc_s[0:HEAD_DIM, cols] / acc_s[HEAD_DIM:HEAD_DIM + 1, cols]
            gate = g_ref[:, g * HEAD_DIM:(g + 1) * HEAD_DIM].astype(F32)
            o_ref[:, g * HEAD_DIM:(g + 1) * HEAD_DIM] = (out_t.T * gate).astype(BF16)


def _attn_global(p_lat, p_ctx, qt, vt_lat, vt_ctx, bq, bk):
    n = p_lat.shape[0]
    hd = HEAD_DIM
    qw = Q_PER_KV * hd
    ones_rows = ONES_ROWS
    k_all = jnp.concatenate([p_lat[:, SLOT_AK * hd:SLOT_BK * hd], p_ctx[:, SLOT_AK * hd:SLOT_BK * hd]], axis=0)
    vt_ext = jnp.concatenate([vt_lat, vt_ctx], axis=1)
    n_kv = k_all.shape[0]
    assert n_kv % bk == 0 and n % bq == 0
    nk = n_kv // bk
    nq = n // bq
    width = Q_PER_KV * bq
    return pl.pallas_call(
        functools.partial(_attn_global_kernel, bq=bq, nk=nk, tq=512),
        out_shape=jax.ShapeDtypeStruct((n, Q_HEADS * hd), BF16),
        grid=(KV_HEADS, nq, nk),
        in_specs=[
            pl.BlockSpec((qw, bq), lambda h, i, k: (h, i)),
            pl.BlockSpec((qw, bq), lambda h, i, k: (h, jnp.minimum(i + 1, nq - 1))),
            pl.BlockSpec((bk, hd), lambda h, i, k: (0, h)),
            pl.BlockSpec((bk, hd), lambda h, i, k: ((k + 1) % nk, h)),
            pl.BlockSpec((hd + ones_rows, bk), lambda h, i, k: (h, k)),
            pl.BlockSpec((bq, qw), lambda h, i, k: (i, SLOT_AG // Q_PER_KV + h)),
        ],
        out_specs=pl.BlockSpec((bq, qw), lambda h, i, k: (i, h)),
        scratch_shapes=[
            pltpu.VMEM((qw, bq), BF16),
            pltpu.VMEM((bk, width), F32),
            pltpu.VMEM((1, width), F32),
            pltpu.VMEM((1, width), F32),
            pltpu.VMEM((hd + ones_rows, width), F32),
        ],
        compiler_params=_params(("arbitrary", "arbitrary", "arbitrary")),
        name="attn_global",
    )(qt, qt, k_all, k_all, vt_ext, p_lat)


def _sink_column(sink_ref, base, rows_per_head, t=None):
    row = lax.broadcasted_iota(jnp.int32, (Q_PER_KV * rows_per_head, 1), 0)
    get = (lambda g: sink_ref[base + g]) if t is None else (lambda g: sink_ref[t, base + g])
    get = functools.partial(lambda f, g: f(g) * LOG2_E, get)
    return jnp.where(row < rows_per_head, get(0), jnp.where(row < 2 * rows_per_head, get(1), get(2)))


def _attn_window_kernel(sink_ref, q_ref, kp_ref, km_ref, kn_ref, vp_ref, vm_ref, vn_ref, kc_ref, vc_ref,
                        g_ref, o_ref, *, bq, n):
    h = pl.program_id(0)
    qi = pl.program_id(1)
    kcat = jnp.concatenate([kp_ref[...], km_ref[...], kn_ref[...]], axis=0)
    vcat = jnp.concatenate([vp_ref[...], vm_ref[...], vn_ref[...]], axis=0)
    kc = kc_ref[...]
    vc = vc_ref[...]
    span = 3 * WINDOW
    rows = Q_PER_KV * WINDOW
    r = lax.broadcasted_iota(jnp.int32, (rows, span), 0) & (WINDOW - 1)
    cidx = lax.broadcasted_iota(jnp.int32, (rows, span), 1)
    rel = cidx - r
    band = (rel >= 0) & (rel <= 2 * WINDOW)
    sink = _sink_column(sink_ref, h * Q_PER_KV, WINDOW)
    n_sub = bq // WINDOW
    q3s = [_stack_heads(q_ref, slice(sb * WINDOW, (sb + 1) * WINDOW)) for sb in range(n_sub)]
    band_scores = [_dot_nt(q3s[sb], kcat[sb * WINDOW:sb * WINDOW + span]) for sb in range(n_sub)]
    ctx_scores = [_dot_nt(q3s[sb], kc) for sb in range(n_sub)]
    for sb in range(n_sub):
        vw = vcat[sb * WINDOW:sb * WINDOW + span]
        key_pos = qi * bq + (sb - 1) * WINDOW + cidx
        valid = band & (key_pos >= 0) & (key_pos < n)
        s = jnp.where(valid, band_scores[sb], NEG_INF)
        sc = ctx_scores[sb]
        m = jnp.maximum(jnp.maximum(jnp.max(s, axis=-1, keepdims=True), jnp.max(sc, axis=-1, keepdims=True)), sink)
        p = jnp.exp2(s - m)
        pc = jnp.exp2(sc - m)
        den = jnp.sum(p, axis=-1, keepdims=True) + jnp.sum(pc, axis=-1, keepdims=True) + jnp.exp2(sink - m)
        out = (_dot(p.astype(BF16), vw) + _dot(pc.astype(BF16), vc)) / den
        for g in range(Q_PER_KV):
            gate = g_ref[sb * WINDOW:(sb + 1) * WINDOW, g * HEAD_DIM:(g + 1) * HEAD_DIM].astype(F32)
            o_ref[sb * WINDOW:(sb + 1) * WINDOW, g * HEAD_DIM:(g + 1) * HEAD_DIM] = (
                out[g * WINDOW:(g + 1) * WINDOW] * gate).astype(BF16)


def _attn_window(p_lat, p_ctx, sink, bq):
    n = p_lat.shape[0]
    n_ctx = p_ctx.shape[0]
    qw = Q_PER_KV * HEAD_DIM
    per = bq // WINDOW
    last = n // WINDOW - 1
    prev_map = lambda slot: (lambda h, i: (jnp.maximum(i * per - 1, 0), slot + h))
    main_map = lambda slot: (lambda h, i: (i, slot + h))
    next_map = lambda slot: (lambda h, i: (jnp.minimum((i + 1) * per, last), slot + h))
    return pl.pallas_call(
        functools.partial(_attn_window_kernel, bq=bq, n=n),
        out_shape=jax.ShapeDtypeStruct((n, Q_HEADS * HEAD_DIM), BF16),
        grid=(KV_HEADS, n // bq),
        in_specs=[
            pl.BlockSpec(memory_space=pltpu.SMEM),
            pl.BlockSpec((bq, qw), lambda h, i: (i, SLOT_BQ // Q_PER_KV + h)),
            pl.BlockSpec((WINDOW, HEAD_DIM), prev_map(SLOT_BK)),
            pl.BlockSpec((bq, HEAD_DIM), main_map(SLOT_BK)),
            pl.BlockSpec((WINDOW, HEAD_DIM), next_map(SLOT_BK)),
            pl.BlockSpec((WINDOW, HEAD_DIM), prev_map(SLOT_BV)),
            pl.BlockSpec((bq, HEAD_DIM), main_map(SLOT_BV)),
            pl.BlockSpec((WINDOW, HEAD_DIM), next_map(SLOT_BV)),
            pl.BlockSpec((n_ctx, HEAD_DIM), lambda h, i: (0, SLOT_BK + h)),
            pl.BlockSpec((n_ctx, HEAD_DIM), lambda h, i: (0, SLOT_BV + h)),
            pl.BlockSpec((bq, qw), lambda h, i: (i, SLOT_BG // Q_PER_KV + h)),
        ],
        out_specs=pl.BlockSpec((bq, qw), lambda h, i: (i, h)),
        compiler_params=_params(("arbitrary", "arbitrary")),
        name="attn_window",
    )(sink, p_lat, p_lat, p_lat, p_lat, p_lat, p_lat, p_lat, p_ctx, p_ctx, p_lat)


def _attn_ctx_kernel(sink_ref, q_ref, k_ref, v_ref, g_ref, o_ref, *, n_ctx):
    idx = pl.program_id(0)
    t = idx // KV_HEADS
    h = idx % KV_HEADS
    q3 = _stack_heads(q_ref)
    s = _dot_nt(q3, k_ref[...])
    sink = _sink_column(sink_ref, h * Q_PER_KV, n_ctx, t=t)
    m = jnp.maximum(jnp.max(s, axis=-1, keepdims=True), sink)
    p = jnp.exp2(s - m)
    den = jnp.sum(p, axis=-1, keepdims=True) + jnp.exp2(sink - m)
    out = _dot(p.astype(BF16), v_ref[...]) / den
    for g in range(Q_PER_KV):
        gate = g_ref[:, g * HEAD_DIM:(g + 1) * HEAD_DIM].astype(F32)
        o_ref[:, g * HEAD_DIM:(g + 1) * HEAD_DIM] = (out[g * n_ctx:(g + 1) * n_ctx] * gate).astype(BF16)


def _attn_ctx(p_ctx, sinks):
    n_ctx = p_ctx.shape[0]
    qw = Q_PER_KV * HEAD_DIM
    return pl.pallas_call(
        functools.partial(_attn_ctx_kernel, n_ctx=n_ctx),
        out_shape=jax.ShapeDtypeStruct((n_ctx, 2 * Q_HEADS * HEAD_DIM), BF16),
        grid=(2 * KV_HEADS,),
        in_specs=[
            pl.BlockSpec(memory_space=pltpu.SMEM),
            pl.BlockSpec((n_ctx, qw), lambda i: (0, i)),
            pl.BlockSpec((n_ctx, HEAD_DIM), lambda i: (0, SLOT_AK + i)),
            pl.BlockSpec((n_ctx, HEAD_DIM), lambda i: (0, SLOT_AV + i)),
            pl.BlockSpec((n_ctx, qw), lambda i: (0, SLOT_AG // Q_PER_KV + i)),
        ],
        out_specs=pl.BlockSpec((n_ctx, qw), lambda i: (0, i)),
        compiler_params=_params(("arbitrary",)),
        name="attn_ctx",
    )(sinks, p_ctx, p_ctx, p_ctx, p_ctx)


def _complex_mul(a_re, a_im, b_re, b_im):
    return a_re * b_re - a_im * b_im, a_re * b_im + a_im * b_re


def _dot_split(a, b):
    a_hi = a.astype(BF16)
    b_hi = b.astype(BF16)
    a_lo = (a - a_hi.astype(F32)).astype(BF16)
    b_lo = (b - b_hi.astype(F32)).astype(BF16)
    return _dot(a_hi, b_hi) + _dot(a_hi, b_lo) + _dot(a_lo, b_hi)


def _ssm_prep_kernel(lam_r_ref, bd_b_ref, bd_c_ref, m_ref, w_ref, v_ref, sc_ref):
    T = SSM_T
    row8 = lax.broadcasted_iota(jnp.int32, (SCAN_ROWS, 1), 0)
    n_exp = T + 1 + 3 + SCAN_ROWS
    n_col = -(-(T + 1) // SCAN_ROWS) * SCAN_ROWS
    erow = lax.broadcasted_iota(jnp.int32, (n_exp, 1), 0)
    lag_kernels = []
    for d in range(2):
        lam_re, lam_im = lam_r_ref[d, 0:1, :], lam_r_ref[d, 1:2, :]
        step = jnp.exp(lam_r_ref[d, 2:3, :])
        dist = erow - (T + 4)
        dist = dist if d == 0 else SCAN_ROWS - 1 - dist
        expo = jnp.where(erow <= T, erow,
                         jnp.where(erow == T + 1, 2 * T,
                                   jnp.where(erow == T + 2, 4 * T,
                                             jnp.where(erow == T + 3, 8 * T, T * dist)))).astype(F32)
        mag = jnp.exp(lam_re * step * expo)
        ang = lam_im * step * expo
        p_re, p_im = mag * jnp.cos(ang), mag * jnp.sin(ang)
        pc_re, pc_im = p_re[0:n_col].T, p_im[0:n_col].T

        nr, ni = p_re[1:2] - 1.0, p_im[1:2]
        den = lam_re * lam_re + lam_im * lam_im
        q_re = (nr * lam_re + ni * lam_im) / den
        q_im = (ni * lam_re - nr * lam_im) / den
        bb_re, bb_im = _complex_mul(bd_b_ref[d, 0], bd_b_ref[d, 1], q_re, q_im)
        c_re, c_im = bd_c_ref[d, 0], bd_c_ref[d, 1]

        def w_rows(e):
            g_re, g_im = _complex_mul(bb_re, bb_im, p_re[e:e + 1], p_im[e:e + 1])
            return jnp.concatenate([g_re, g_im], axis=1)

        def v_cols(e):
            g_re, g_im = _complex_mul(c_re, c_im, pc_re[:, e:e + 1], pc_im[:, e:e + 1])
            return jnp.concatenate([g_re, -g_im], axis=0)

        w = jnp.concatenate([w_rows(T - 1 - t if d == 0 else t) for t in range(T)], axis=0)
        w_ref[d] = w.astype(BF16)
        v_ref[d] = jnp.concatenate([v_cols(t + 1 if d == 0 else T - t) for t in range(T)], axis=1).astype(BF16)
        lags = _dot_split(w, v_cols(0))
        lag_of_block = [(T - 1 - t if d == 0 else t) for t in range(T)]
        lag_kernels.append({lag: lags[t * HEAD_DIM:(t + 1) * HEAD_DIM] for t, lag in enumerate(lag_of_block)})

        ahead = (lambda k: row8 >= k) if d == 0 else (lambda k: row8 <= SCAN_ROWS - 1 - k)
        rows_of = {1: T, 2: T + 1, 4: T + 2, 8: T + 3}
        parts_re, parts_im = [], []
        for k in (1, 2, 4):
            r = rows_of[k]
            parts_re.append(jnp.where(ahead(k), p_re[r:r + 1], 0.0))
            parts_im.append(jnp.where(ahead(k), p_im[r:r + 1], 0.0))
        parts_re.append(p_re[T + 4:T + 4 + SCAN_ROWS])
        parts_im.append(p_im[T + 4:T + 4 + SCAN_ROWS])
        r = rows_of[8]
        parts_re.append(jnp.broadcast_to(p_re[r:r + 1], (SCAN_ROWS, p_re.shape[1])))
        parts_im.append(jnp.broadcast_to(p_im[r:r + 1], (SCAN_ROWS, p_im.shape[1])))
        sc_ref[d, 0] = jnp.concatenate(parts_re, axis=0)
        sc_ref[d, 1] = jnp.concatenate(parts_im, axis=0)

    fwd, bwd = lag_kernels
    for t in range(T):
        blocks = []
        for t2 in range(T):
            if t2 > t:
                blocks.append(fwd[t2 - t])
            elif t2 < t:
                blocks.append(bwd[t - t2])
            else:
                blocks.append(fwd[0] + bwd[0])
        m_ref[t * HEAD_DIM:(t + 1) * HEAD_DIM, :] = jnp.concatenate(blocks, axis=1).astype(BF16)


def _ssm_prep(lam_re, lam_im, log_step, b_re, b_im, c_re, c_im):
    n_dir, groups, state = lam_re.shape
    bg = SSM_BLOCK_GROUPS
    nb = groups // bg
    ns = SSM_BLOCK_STATES
    width = SSM_T * HEAD_DIM
    step = jnp.broadcast_to(log_step[:, :, None], lam_re.shape)
    rows = jnp.stack([lam_re, lam_im, step], axis=1)
    lam_r = rows.reshape(n_dir, 3, nb, ns).transpose(2, 0, 1, 3)
    eye = jnp.eye(bg, dtype=F32)

    def b_layout(b):
        bt = b.reshape(n_dir, nb, bg, state, SSM_GROUP)
        return jnp.einsum('dbgpj,gh->bdgjhp', bt, eye).reshape(nb, n_dir, HEAD_DIM, ns)

    def c_layout(c):
        ct = c.reshape(n_dir, nb, bg, SSM_GROUP, state)
        return jnp.einsum('dbhip,gh->bdhpgi', ct, eye).reshape(nb, n_dir, ns, HEAD_DIM)

    bd_b = jnp.stack([b_layout(b_re), b_layout(b_im)], axis=2)
    bd_c = jnp.stack([c_layout(c_re), c_layout(c_im)], axis=2)
    return pl.pallas_call(
        _ssm_prep_kernel,
        out_shape=(
            jax.ShapeDtypeStruct((nb, width, width), BF16),
            jax.ShapeDtypeStruct((nb, n_dir, width, 2 * ns), BF16),
            jax.ShapeDtypeStruct((nb, n_dir, 2 * ns, width), BF16),
            jax.ShapeDtypeStruct((nb, n_dir, 2, 5 * SCAN_ROWS, ns), F32),
        ),
        grid=(nb,),
        in_specs=[
            pl.BlockSpec((None, n_dir, 3, ns), lambda b: (b, 0, 0, 0)),
            pl.BlockSpec((None, n_dir, 2, HEAD_DIM, ns), lambda b: (b, 0, 0, 0, 0)),
            pl.BlockSpec((None, n_dir, 2, ns, HEAD_DIM), lambda b: (b, 0, 0, 0, 0)),
        ],
        out_specs=(
            pl.BlockSpec((None, width, width), lambda b: (b, 0, 0)),
            pl.BlockSpec((None, n_dir, width, 2 * ns), lambda b: (b, 0, 0, 0)),
            pl.BlockSpec((None, n_dir, 2 * ns, width), lambda b: (b, 0, 0, 0)),
            pl.BlockSpec((None, n_dir, 2, 5 * SCAN_ROWS, ns), lambda b: (b, 0, 0, 0, 0)),
        ),
        compiler_params=_params(("arbitrary",)),
        name="ssm_prep",
    )(lam_r, bd_b, bd_c)


def _chunk_rows(u_ref, rb):
    return jnp.concatenate([u_ref[pl.ds(t, rb, stride=SSM_T), :] for t in range(SSM_T)], axis=1).astype(BF16)


def _ssm_states_kernel(uf_ref, ub_ref, w_ref, sc_ref, cin_ref, hf_ref, hb_ref, cout_ref, s_s, carry_s, *, rb, nrb):
    step = pl.program_id(1)
    ns = SSM_BLOCK_STATES
    row8 = lax.broadcasted_iota(jnp.int32, (SCAN_ROWS, 1), 0)

    @pl.when(step == 0)
    def _():
        carry_s[...] = cin_ref[...]

    for d, (u_ref, h_ref) in enumerate(((uf_ref, hf_ref), (ub_ref, hb_ref))):
        s_s[d] = _dot(_chunk_rows(u_ref, rb), w_ref[d])
        inner = (row8 >= 1) if d == 0 else (row8 <= SCAN_ROWS - 2)
        em = jnp.where(inner, 1.0, 0.0).astype(F32)
        consts = [(sc_ref[d, 0, i * SCAN_ROWS:(i + 1) * SCAN_ROWS, :], sc_ref[d, 1, i * SCAN_ROWS:(i + 1) * SCAN_ROWS, :])
                  for i in range(5)]
        n_groups = rb // SCAN_ROWS

        def body(g, carry, d=d, em=em, consts=consts):
            c_re, c_im = carry
            gi = g if d == 0 else n_groups - 1 - g
            rows = pl.ds(pl.multiple_of(gi * SCAN_ROWS, SCAN_ROWS), SCAN_ROWS)
            p_re = s_s[d, rows, 0:ns]
            p_im = s_s[d, rows, ns:2 * ns]
            for i, k in enumerate((1, 2, 4)):
                shift = k if d == 0 else SCAN_ROWS - k
                m_re, m_im = _complex_mul(consts[i][0], consts[i][1], pltpu.roll(p_re, shift, 0), pltpu.roll(p_im, shift, 0))
                p_re, p_im = p_re + m_re, p_im + m_im
            shift = 1 if d == 0 else SCAN_ROWS - 1
            k_re, k_im = _complex_mul(consts[3][0], consts[3][1], c_re, c_im)
            s_s[d, rows, 0:ns] = em * pltpu.roll(p_re, shift, 0) + k_re
            s_s[d, rows, ns:2 * ns] = em * pltpu.roll(p_im, shift, 0) + k_im
            last = SCAN_ROWS - 1 if d == 0 else 0
            n_re, n_im = _complex_mul(consts[4][0][0:1], consts[4][1][0:1], c_re, c_im)
            return p_re[last:last + 1] + n_re, p_im[last:last + 1] + n_im

        c_re, c_im = lax.fori_loop(0, n_groups, body, (carry_s[d, 0, 0:1, :], carry_s[d, 1, 0:1, :]), unroll=True)
        carry_s[d, 0, 0:1, :] = c_re
        carry_s[d, 1, 0:1, :] = c_im
        h_ref[...] = s_s[d].astype(BF16)

    @pl.when(step == nrb - 1)
    def _():
        cout_ref[...] = carry_s[...]


def _ssm_states(u, w, sc, carry_in):
    n_tok, width = u.shape
    nb = width // HEAD_DIM
    n_chunks = n_tok // SSM_T
    rb = min(SSM_ROW_BLOCK, n_chunks)
    nrb = n_chunks // rb
    ns = SSM_BLOCK_STATES
    h_shape = jax.ShapeDtypeStruct((n_chunks, nb * 2 * ns), BF16)
    return pl.pallas_call(
        functools.partial(_ssm_states_kernel, rb=rb, nrb=nrb),
        out_shape=(h_shape, h_shape, jax.ShapeDtypeStruct(carry_in.shape, F32)),
        grid=(nb, nrb),
        in_specs=[
            pl.BlockSpec((rb * SSM_T, HEAD_DIM), lambda b, r: (r, b)),
            pl.BlockSpec((rb * SSM_T, HEAD_DIM), lambda b, r: (nrb - 1 - r, b)),
            pl.BlockSpec((None, 2, SSM_T * HEAD_DIM, 2 * ns), lambda b, r: (b, 0, 0, 0)),
            pl.BlockSpec((None, 2, 2, 5 * SCAN_ROWS, ns), lambda b, r: (b, 0, 0, 0, 0)),
            pl.BlockSpec((None, 2, 2, SCAN_ROWS, ns), lambda b, r: (b, 0, 0, 0, 0)),
        ],
        out_specs=(
            pl.BlockSpec((rb, 2 * ns), lambda b, r: (r, b)),
            pl.BlockSpec((rb, 2 * ns), lambda b, r: (nrb - 1 - r, b)),
            pl.BlockSpec((None, 2, 2, SCAN_ROWS, ns), lambda b, r: (b, 0, 0, 0, 0)),
        ),
        scratch_shapes=[pltpu.VMEM((2, rb, 2 * ns), F32), pltpu.VMEM((2, 2, SCAN_ROWS, ns), F32)],
        compiler_params=_params(("arbitrary", "arbitrary")),
        name="ssm_states",
    )(u, u, w, sc, carry_in)


def _ssm_y_kernel(u_ref, hf_ref, hb_ref, m_ref, v_ref, y_ref, *, rb):
    ycat = (_dot(_chunk_rows(u_ref, rb), m_ref[...]) + _dot(hf_ref[...], v_ref[0]) + _dot(hb_ref[...], v_ref[1]))
    for t in range(SSM_T):
        y_ref[pl.ds(t, rb, stride=SSM_T), :] = ycat[:, t * HEAD_DIM:(t + 1) * HEAD_DIM]


def _ssm_y(u, hf, hb, m, v):
    n_tok, width = u.shape
    nb = width // HEAD_DIM
    n_chunks = n_tok // SSM_T
    rb = min(SSM_ROW_BLOCK, n_chunks)
    ns = SSM_BLOCK_STATES
    cw = SSM_T * HEAD_DIM
    return pl.pallas_call(
        functools.partial(_ssm_y_kernel, rb=rb),
        out_shape=jax.ShapeDtypeStruct((n_tok, width), F32),
        grid=(nb, n_chunks // rb),
        in_specs=[
            pl.BlockSpec((rb * SSM_T, HEAD_DIM), lambda b, r: (r, b)),
            pl.BlockSpec((rb, 2 * ns), lambda b, r: (r, b)),
            pl.BlockSpec((rb, 2 * ns), lambda b, r: (r, b)),
            pl.BlockSpec((None, cw, cw), lambda b, r: (b, 0, 0)),
            pl.BlockSpec((None, 2, 2 * ns, cw), lambda b, r: (b, 0, 0, 0)),
        ],
        out_specs=pl.BlockSpec((rb * SSM_T, HEAD_DIM), lambda b, r: (r, b)),
        compiler_params=_params(("arbitrary", "arbitrary")),
        name="ssm_y",
    )(u, hf, hb, m, v)


def _outproj_kernel(*refs, n_attn, final):
    mix = refs[:n_attn]
    y_ref, u_ref, sg_ref, d_ref, wg_ref, bg_ref, w_ref, x_ref, gate_ref = refs[n_attn:n_attn + 9]
    rest = refs[n_attn + 9:]
    y = y_ref[...] + d_ref[...] * u_ref[...].astype(F32)
    y = jax.nn.gelu(y, approximate=True)
    z = _dot(y.astype(BF16), wg_ref[...]) + bg_ref[...]
    mix_s = (y * jax.nn.sigmoid(z) * sg_ref[...].astype(F32)).astype(BF16)
    acc = None
    start = 0
    for a in [m[...] for m in mix] + [mix_s]:
        part = _dot(a, w_ref[start:start + a.shape[1], :])
        acc = part if acc is None else acc + part
        start += a.shape[1]
    x = x_ref[...] + gate_ref[...] * acc
    if final:
        fg_ref, o_ref = rest
        x = x * lax.rsqrt(jnp.mean(x * x, axis=-1, keepdims=True) + NORM_EPS) * fg_ref[...]
    else:
        (o_ref,) = rest
    o_ref[...] = x


def _outproj(attn_mixes, y, p, d_skip, w_glu, b_glu, w_out, x, gate, final_gain, bm):
    n, d = x.shape
    s_width = y.shape[1]
    blk = s_width // HEAD_DIM
    row = lambda i: (i, 0)
    const = lambda i: (0, 0)
    in_specs = [pl.BlockSpec((bm, a.shape[1]), row) for a in attn_mixes]
    in_specs += [
        pl.BlockSpec((bm, s_width), row),
        pl.BlockSpec((bm, s_width), lambda i: (i, SLOT_SU // blk)),
        pl.BlockSpec((bm, s_width), lambda i: (i, SLOT_SG // blk)),
        pl.BlockSpec((1, s_width), const),
        pl.BlockSpec((s_width, s_width), const),
        pl.BlockSpec((1, s_width), const),
        pl.BlockSpec(w_out.shape, const, pipeline_mode=pl.Buffered(1)),
        pl.BlockSpec((bm, d), row),
        pl.BlockSpec((1, d), const),
    ]
    args = list(attn_mixes) + [y, p, p, d_skip, w_glu, b_glu, w_out, x, gate]
    final = final_gain is not None
    if final:
        in_specs.append(pl.BlockSpec((1, d), const))
        args.append(final_gain)
    return pl.pallas_call(
        functools.partial(_outproj_kernel, n_attn=len(attn_mixes), final=final),
        out_shape=jax.ShapeDtypeStruct((n, d), F32),
        grid=(n // bm,),
        in_specs=in_specs,
        out_specs=pl.BlockSpec((bm, d), row),
        compiler_params=_params(("arbitrary",)),
        name="outproj_final" if final else "outproj",
    )(*args)


def _rope_tables(n):
    pos = np.arange(n)
    half = HEAD_DIM // 4
    freqs = (ROPE_THETA ** (-np.arange(half, dtype=np.float32) / half)).astype(np.float32)
    ang_r = ((pos // GRID_W).astype(np.float32)[:, None] * freqs[None, :]).astype(np.float64)
    ang_c = ((pos % GRID_W).astype(np.float32)[:, None] * freqs[None, :]).astype(np.float64)
    cos = np.concatenate([np.cos(ang_r)] * 2 + [np.cos(ang_c)] * 2, axis=-1)
    sin = np.concatenate([-np.sin(ang_r), np.sin(ang_r), -np.sin(ang_c), np.sin(ang_c)], axis=-1)
    return jnp.asarray(cos, F32), jnp.asarray(sin, F32)


def _reorder_w_in(w):
    hd = HEAD_DIM
    sizes = (6 * hd, 2 * hd, 2 * hd, 6 * hd, 6 * hd, 2 * hd, 2 * hd, 6 * hd, 4 * hd, 4 * hd)
    names = ("aq", "ak", "av", "ag", "bq", "bk", "bv", "bg", "su", "sg")
    parts, start = {}, 0
    for name, size in zip(names, sizes):
        parts[name] = w[:, start:start + size]
        start += size
    order = ("aq", "bq", "ak", "bk", "av", "bv", "su", "ag", "bg", "sg")
    return jnp.concatenate([parts[k] for k in order], axis=1).astype(BF16)


def kernel(x, c, ctx, c_ctx, w_ada, b_ada, norm_gain, w_in, a_q_gain, a_k_gain, b_sink, ssm_lambda_re,
           ssm_lambda_im, ssm_log_step, ssm_b_re, ssm_b_im, ssm_c_re, ssm_c_im, ssm_d, w_glu, b_glu, w_out,
           final_gain):
    depth, d = norm_gain.shape
    n = x.shape[1]
    n_ctx = ctx.shape[1]
    assert x.shape[0] == 1 and n % 512 == 0 and n_ctx % (SSM_T * SCAN_ROWS) == 0
    xs = x[0]
    cs = ctx[0]
    bm = 512

    c_t = jnp.concatenate([c.reshape(d, 1), c_ctx.reshape(d, 1)], axis=1)
    mod = _modulation(c_t, w_ada, b_ada)
    tables = _rope_tables(n)

    for layer in range(depth):
        last = layer == depth - 1
        shift, scale, gate = (mod[layer, :, i * d:(i + 1) * d] for i in range(3))
        w = _reorder_w_in(w_in[layer])
        gain = norm_gain[layer].reshape(1, d)
        qk_gain = jnp.stack([a_q_gain[layer], a_k_gain[layer]], axis=0)
        p_lat, qt_lat, vt_lat, su_lat = _inproj(xs, shift[0:1], scale[0:1], gain, w, qk_gain, tables, bm)
        p_ctx, _, vt_ctx, su_ctx = _inproj(cs, shift[1:2], scale[1:2], gain, w, qk_gain, None, n_ctx)

        mix_a = _attn_global(p_lat, p_ctx, qt_lat, vt_lat, vt_ctx, bq=_largest_block(n, 2048),
                             bk=_largest_block(n + n_ctx, 1280))
        mix_b = _attn_window(p_lat, p_ctx, b_sink[layer], bq=_largest_block(n, 1024))

        m_op, w_op, v_op, scan_c = _ssm_prep(ssm_lambda_re[layer], ssm_lambda_im[layer], ssm_log_step[layer],
                                            ssm_b_re[layer], ssm_b_im[layer], ssm_c_re[layer], ssm_c_im[layer])
        no_state = jnp.zeros((su_lat.shape[1] // HEAD_DIM, 2, 2, SCAN_ROWS, SSM_BLOCK_STATES), F32)
        hf_ctx, hb_ctx, ctx_state = _ssm_states(su_ctx, w_op, scan_c, no_state)
        hf_lat, hb_lat, _ = _ssm_states(su_lat, w_op, scan_c, ctx_state)
        y_lat = _ssm_y(su_lat, hf_lat, hb_lat, m_op, v_op)
        d_skip = ssm_d[layer].reshape(1, -1)
        wg = w_glu[layer].astype(BF16)
        bg = b_glu[layer].reshape(1, -1)
        wo = w_out[layer].astype(BF16)
        xs_new = _outproj((mix_a, mix_b), y_lat, p_lat, d_skip, wg, bg, wo, xs, gate[0:1],
                          final_gain.reshape(1, d) if last else None, bm)
        if not last:
            sinks = jnp.stack([jnp.full_like(b_sink[layer], NEG_INF), b_sink[layer]], axis=0)
            mix_ab_c = _attn_ctx(p_ctx, sinks)
            y_ctx = _ssm_y(su_ctx, hf_ctx, hb_ctx, m_op, v_op)
            cs = _outproj((mix_ab_c,), y_ctx, p_ctx, d_skip, wg, bg, wo, cs, gate[1:2], None, n_ctx)
        xs = xs_new
    return xs[None]
```

```python
import functools
import math

import jax
import jax.numpy as jnp
import numpy as np
from jax import lax
from jax.experimental import pallas as pl
from jax.experimental.pallas import tpu as pltpu

F32 = jnp.float32
BF16 = jnp.bfloat16

HEAD_DIM = 128
GRID_W = 64
Q_PER_KV = 3
KV_HEADS = 2
Q_HEADS = Q_PER_KV * KV_HEADS
WINDOW = 128
SSM_GROUP = 16
SSM_STATE = 64
SSM_T = 8
SSM_BLOCK_GROUPS = HEAD_DIM // SSM_GROUP
SSM_BLOCK_STATES = SSM_BLOCK_GROUPS * SSM_STATE
SCAN_ROWS = 8
SSM_ROW_BLOCK = 512
ONES_ROWS = 16
ROPE_THETA = 10000.0
NORM_EPS = 1e-6
NEG_INF = -1e30
LOG2_E = math.log2(math.e)
VMEM_LIMIT_V7X = 56 * 1024 * 1024

SLOT_AQ, SLOT_BQ, SLOT_AK, SLOT_BK, SLOT_AV, SLOT_BV = 0, 6, 12, 14, 16, 18
SLOT_SU, SLOT_AG, SLOT_BG, SLOT_SG = 20, 24, 30, 36
N_SLOTS = 40
PANEL_SLOTS = 8


def _params(sem, vmem=VMEM_LIMIT_V7X):
    return pltpu.CompilerParams(dimension_semantics=sem, vmem_limit_bytes=vmem)


def _largest_block(total, limit):
    return max(b for b in range(HEAD_DIM, limit + 1, HEAD_DIM) if total % b == 0)


def _dot(a, b):
    return jnp.dot(a, b, preferred_element_type=F32)


def _dot_nt(a, b):
    return lax.dot_general(a, b, (((1,), (1,)), ((), ())), preferred_element_type=F32)


def _silu(x):
    return x * jax.nn.sigmoid(x)


def _mod_kernel(ct_ref, w_ref, b_ref, o_ref):
    s = _silu(ct_ref[...])
    w = w_ref[...]
    b = b_ref[...]
    o_ref[0:1, :] = jnp.sum(s[:, 0:1] * w, axis=0, keepdims=True) + b
    o_ref[1:2, :] = jnp.sum(s[:, 1:2] * w, axis=0, keepdims=True) + b


def _modulation(c_t, w_ada, b_ada):
    depth, d, n3 = w_ada.shape
    tn = 1024
    return pl.pallas_call(
        _mod_kernel,
        out_shape=jax.ShapeDtypeStruct((depth, 2, n3), F32),
        grid=(depth, n3 // tn),
        in_specs=[
            pl.BlockSpec((d, 2), lambda l, j: (0, 0)),
            pl.BlockSpec((None, d, tn), lambda l, j: (l, 0, j)),
            pl.BlockSpec((None, 1, tn), lambda l, j: (l, 0, j)),
        ],
        out_specs=pl.BlockSpec((None, 2, tn), lambda l, j: (l, 0, j)),
        compiler_params=_params(("arbitrary", "arbitrary")),
        name="adaln_modulation",
    )(c_t, w_ada, b_ada.reshape(depth, 1, n3))


def _slot_config(slot):
    scale = HEAD_DIM ** -0.5 * LOG2_E
    if slot < SLOT_BQ:
        return 0, True, scale, False
    if slot < SLOT_AK:
        return None, True, scale, False
    if slot < SLOT_BK:
        return 1, True, None, False
    if slot < SLOT_AV:
        return None, True, None, False
    if slot < SLOT_AG:
        return None, False, None, False
    return None, False, None, True


def _inproj_kernel(*refs, rope):
    if rope:
        x_ref, shift_ref, scale_ref, gain_ref, w_ref, qk_ref, cos_ref, sin_ref, o_ref, qt_ref, vt_ref, su_ref = refs
    else:
        x_ref, shift_ref, scale_ref, gain_ref, w_ref, qk_ref, o_ref, qt_ref, vt_ref, su_ref = refs
    x = x_ref[...]
    y = x * lax.rsqrt(jnp.mean(x * x, axis=-1, keepdims=True) + NORM_EPS) * gain_ref[...]
    h = (y * (1.0 + scale_ref[...]) + shift_ref[...]).astype(BF16)
    lane = lax.broadcasted_iota(jnp.int32, (1, HEAD_DIM), 1)
    first_half = (lane & 63) < 32
    bn = PANEL_SLOTS * HEAD_DIM
    vt_rows = HEAD_DIM + ONES_ROWS

    for panel in range(N_SLOTS // PANEL_SLOTS):
        acc = _dot(h, w_ref[:, panel * bn:(panel + 1) * bn])
        for k in range(PANEL_SLOTS):
            slot = panel * PANEL_SLOTS + k
            gain_row, rotary, scale, act = _slot_config(slot)
            t = acc[:, k * HEAD_DIM:(k + 1) * HEAD_DIM]
            if gain_row is not None:
                t = (t * lax.rsqrt(jnp.mean(t * t, axis=-1, keepdims=True) + NORM_EPS)
                     * qk_ref[gain_row:gain_row + 1, :])
            if rotary and rope:
                partner = jnp.where(first_half, pltpu.roll(t, 96, 1), pltpu.roll(t, 32, 1))
                t = t * cos_ref[...] + partner * sin_ref[...]
            if scale is not None:
                t = t * scale
            if act:
                t = _silu(t)
            o_ref[:, slot * HEAD_DIM:(slot + 1) * HEAD_DIM] = t.astype(BF16)
            if SLOT_AQ <= slot < SLOT_BQ:
                qt_ref[(slot - SLOT_AQ) * HEAD_DIM:(slot - SLOT_AQ + 1) * HEAD_DIM, :] = t.T.astype(BF16)
            if SLOT_AV <= slot < SLOT_BV:
                base = (slot - SLOT_AV) * vt_rows
                vt_ref[base:base + HEAD_DIM, :] = t.T.astype(BF16)
                vt_ref[base + HEAD_DIM:base + vt_rows, :] = jnp.ones((ONES_ROWS, t.shape[0]), BF16)
            if SLOT_SU <= slot < SLOT_AG:
                su_ref[:, (slot - SLOT_SU) * HEAD_DIM:(slot - SLOT_SU + 1) * HEAD_DIM] = t


def _inproj(x, shift, scale, gain, w, qk_gain, rope_tables, bm):
    n, d = x.shape
    width = w.shape[1]
    rope = rope_tables is not None
    row = lambda i: (i, 0)
    col = lambda i: (0, i)
    const = lambda i: (0, 0)
    in_specs = [
        pl.BlockSpec((bm, d), row),
        pl.BlockSpec((1, d), const),
        pl.BlockSpec((1, d), const),
        pl.BlockSpec((1, d), const),
        pl.BlockSpec((d, width), const, pipeline_mode=pl.Buffered(1)),
        pl.BlockSpec((2, HEAD_DIM), const),
    ]
    args = [x, shift, scale, gain, w, qk_gain]
    if rope:
        in_specs += [pl.BlockSpec((bm, HEAD_DIM), row), pl.BlockSpec((bm, HEAD_DIM), row)]
        args += list(rope_tables)
    qt_rows = Q_HEADS * HEAD_DIM
    vt_rows = KV_HEADS * (HEAD_DIM + ONES_ROWS)
    su_width = (SLOT_AG - SLOT_SU) * HEAD_DIM
    return pl.pallas_call(
        functools.partial(_inproj_kernel, rope=rope),
        out_shape=(
            jax.ShapeDtypeStruct((n, width), BF16),
            jax.ShapeDtypeStruct((qt_rows, n), BF16),
            jax.ShapeDtypeStruct((vt_rows, n), BF16),
            jax.ShapeDtypeStruct((n, su_width), F32),
        ),
        grid=(n // bm,),
        in_specs=in_specs,
        out_specs=(
            pl.BlockSpec((bm, width), row),
            pl.BlockSpec((qt_rows, bm), col),
            pl.BlockSpec((vt_rows, bm), col),
            pl.BlockSpec((bm, su_width), row),
        ),
        compiler_params=_params(("arbitrary",)),
        name="inproj_rope" if rope else "inproj_ctx",
    )(*args)


def _stack_heads(ref, rows=None):
    sl = slice(None) if rows is None else rows
    return jnp.concatenate([ref[sl, g * HEAD_DIM:(g + 1) * HEAD_DIM] for g in range(Q_PER_KV)], axis=0)


def _attn_global_kernel(qt_ref, qn_ref, k0_ref, kn_ref, vt_ref, g_ref, o_ref, qa_s, s_s, mb_s, m_s, acc_s,
                        *, bq, nk, tq):
    qi = pl.program_id(1)
    ki = pl.program_id(2)
    tiles = [(g, c) for g in range(Q_PER_KV) for c in range(bq // tq)]

    def lanes(g, c):
        return slice(g * bq + c * tq, g * bq + (c + 1) * tq)

    def scores(k_ref, g, c):
        s = _dot(k_ref[...], qa_s[g * HEAD_DIM:(g + 1) * HEAD_DIM, c * tq:(c + 1) * tq])
        s_s[:, lanes(g, c)] = s
        mb_s[:, lanes(g, c)] = jnp.max(s, axis=0, keepdims=True)

    @pl.when((qi == 0) & (ki == 0))
    def _():
        qa_s[...] = qt_ref[...]
        for g, c in tiles:
            scores(k0_ref, g, c)

    @pl.when(ki == 0)
    def _():
        m_s[...] = jnp.full(m_s.shape, NEG_INF, F32)
        acc_s[...] = jnp.zeros(acc_s.shape, F32)

    @pl.when(ki == nk - 1)
    def _():
        qa_s[...] = qn_ref[...]

    vt = vt_ref[...]
    for g, c in tiles:
        cols = lanes(g, c)
        m_prev = m_s[:, cols]
        m_new = jnp.maximum(m_prev, mb_s[:, cols])
        alpha = jnp.exp2(m_prev - m_new)
        p = jnp.exp2((s_s[:, cols] - m_new).astype(BF16))
        acc_s[:, cols] = alpha * acc_s[:, cols] + _dot(vt, p)
        m_s[:, cols] = m_new
        scores(kn_ref, g, c)

    @pl.when(ki == nk - 1)
    def _():
        for g in range(Q_PER_KV):
            cols = slice(g * bq, (g + 1) * bq)
            out_t = acc_s[0:HEAD_DIM, cols] / acc_s[HEAD_DIM:HEAD_DIM + 1, cols]
            gate = g_ref[:, g * HEAD_DIM:(g + 1) * HEAD_DIM].astype(F32)
            o_ref[:, g * HEAD_DIM:(g + 1) * HEAD_DIM] = (out_t.T * gate).astype(BF16)


def _attn_global(p_lat, p_ctx, qt, vt_lat, vt_ctx, bq, bk):
    n = p_lat.shape[0]
    hd = HEAD_DIM
    qw = Q_PER_KV * hd
    ones_rows = ONES_ROWS
    k_all = jnp.concatenate([p_lat[:, SLOT_AK * hd:SLOT_BK * hd], p_ctx[:, SLOT_AK * hd:SLOT_BK * hd]], axis=0)
    vt_ext = jnp.concatenate([vt_lat, vt_ctx], axis=1)
    n_kv = k_all.shape[0]
    assert n_kv % bk == 0 and n % bq == 0
    nk = n_kv // bk
    nq = n // bq
    width = Q_PER_KV * bq
    return pl.pallas_call(
        functools.partial(_attn_global_kernel, bq=bq, nk=nk, tq=512),
        out_shape=jax.ShapeDtypeStruct((n, Q_HEADS * hd), BF16),
        grid=(KV_HEADS, nq, nk),
        in_specs=[
            pl.BlockSpec((qw, bq), lambda h, i, k: (h, i)),
            pl.BlockSpec((qw, bq), lambda h, i, k: (h, jnp.minimum(i + 1, nq - 1))),
            pl.BlockSpec((bk, hd), lambda h, i, k: (0, h)),
            pl.BlockSpec((bk, hd), lambda h, i, k: ((k + 1) % nk, h)),
            pl.BlockSpec((hd + ones_rows, bk), lambda h, i, k: (h, k)),
            pl.BlockSpec((bq, qw), lambda h, i, k: (i, SLOT_AG // Q_PER_KV + h)),
        ],
        out_specs=pl.BlockSpec((bq, qw), lambda h, i, k: (i, h)),
        scratch_shapes=[
            pltpu.VMEM((qw, bq), BF16),
            pltpu.VMEM((bk, width), F32),
            pltpu.VMEM((1, width), F32),
            pltpu.VMEM((1, width), F32),
            pltpu.VMEM((hd + ones_rows, width), F32),
        ],
        compiler_params=_params(("arbitrary", "arbitrary", "arbitrary")),
        name="attn_global",
    )(qt, qt, k_all, k_all, vt_ext, p_lat)


def _sink_column(sink_ref, base, rows_per_head, t=None):
    row = lax.broadcasted_iota(jnp.int32, (Q_PER_KV * rows_per_head, 1), 0)
    get = (lambda g: sink_ref[base + g]) if t is None else (lambda g: sink_ref[t, base + g])
    get = functools.partial(lambda f, g: f(g) * LOG2_E, get)
    return jnp.where(row < rows_per_head, get(0), jnp.where(row < 2 * rows_per_head, get(1), get(2)))


def _attn_window_kernel(sink_ref, q_ref, kp_ref, km_ref, kn_ref, vp_ref, vm_ref, vn_ref, kc_ref, vc_ref,
                        g_ref, o_ref, *, bq, n):
    h = pl.program_id(0)
    qi = pl.program_id(1)
    kcat = jnp.concatenate([kp_ref[...], km_ref[...], kn_ref[...]], axis=0)
    vcat = jnp.concatenate([vp_ref[...], vm_ref[...], vn_ref[...]], axis=0)
    kc = kc_ref[...]
    vc = vc_ref[...]
    span = 3 * WINDOW
    rows = Q_PER_KV * WINDOW
    r = lax.broadcasted_iota(jnp.int32, (rows, span), 0) & (WINDOW - 1)
    cidx = lax.broadcasted_iota(jnp.int32, (rows, span), 1)
    rel = cidx - r
    band = (rel >= 0) & (rel <= 2 * WINDOW)
    sink = _sink_column(sink_ref, h * Q_PER_KV, WINDOW)
    n_sub = bq // WINDOW
    q3s = [_stack_heads(q_ref, slice(sb * WINDOW, (sb + 1) * WINDOW)) for sb in range(n_sub)]
    band_scores = [_dot_nt(q3s[sb], kcat[sb * WINDOW:sb * WINDOW + span]) for sb in range(n_sub)]
    ctx_scores = [_dot_nt(q3s[sb], kc) for sb in range(n_sub)]
    for sb in range(n_sub):
        vw = vcat[sb * WINDOW:sb * WINDOW + span]
        key_pos = qi * bq + (sb - 1) * WINDOW + cidx
        valid = band & (key_pos >= 0) & (key_pos < n)
        s = jnp.where(valid, band_scores[sb], NEG_INF)
        sc = ctx_scores[sb]
        m = jnp.maximum(jnp.maximum(jnp.max(s, axis=-1, keepdims=True), jnp.max(sc, axis=-1, keepdims=True)), sink)
        p = jnp.exp2(s - m)
        pc = jnp.exp2(sc - m)
        den = jnp.sum(p, axis=-1, keepdims=True) + jnp.sum(pc, axis=-1, keepdims=True) + jnp.exp2(sink - m)
        out = (_dot(p.astype(BF16), vw) + _dot(pc.astype(BF16), vc)) / den
        for g in range(Q_PER_KV):
            gate = g_ref[sb * WINDOW:(sb + 1) * WINDOW, g * HEAD_DIM:(g + 1) * HEAD_DIM].astype(F32)
            o_ref[sb * WINDOW:(sb + 1) * WINDOW, g * HEAD_DIM:(g + 1) * HEAD_DIM] = (
                out[g * WINDOW:(g + 1) * WINDOW] * gate).astype(BF16)


def _attn_window(p_lat, p_ctx, sink, bq):
    n = p_lat.shape[0]
    n_ctx = p_ctx.shape[0]
    qw = Q_PER_KV * HEAD_DIM
    per = bq // WINDOW
    last = n // WINDOW - 1
    prev_map = lambda slot: (lambda h, i: (jnp.maximum(i * per - 1, 0), slot + h))
    main_map = lambda slot: (lambda h, i: (i, slot + h))
    next_map = lambda slot: (lambda h, i: (jnp.minimum((i + 1) * per, last), slot + h))
    return pl.pallas_call(
        functools.partial(_attn_window_kernel, bq=bq, n=n),
        out_shape=jax.ShapeDtypeStruct((n, Q_HEADS * HEAD_DIM), BF16),
        grid=(KV_HEADS, n // bq),
        in_specs=[
            pl.BlockSpec(memory_space=pltpu.SMEM),
            pl.BlockSpec((bq, qw), lambda h, i: (i, SLOT_BQ // Q_PER_KV + h)),
            pl.BlockSpec((WINDOW, HEAD_DIM), prev_map(SLOT_BK)),
            pl.BlockSpec((bq, HEAD_DIM), main_map(SLOT_BK)),
            pl.BlockSpec((WINDOW, HEAD_DIM), next_map(SLOT_BK)),
            pl.BlockSpec((WINDOW, HEAD_DIM), prev_map(SLOT_BV)),
            pl.BlockSpec((bq, HEAD_DIM), main_map(SLOT_BV)),
            pl.BlockSpec((WINDOW, HEAD_DIM), next_map(SLOT_BV)),
            pl.BlockSpec((n_ctx, HEAD_DIM), lambda h, i: (0, SLOT_BK + h)),
            pl.BlockSpec((n_ctx, HEAD_DIM), lambda h, i: (0, SLOT_BV + h)),
            pl.BlockSpec((bq, qw), lambda h, i: (i, SLOT_BG // Q_PER_KV + h)),
        ],
        out_specs=pl.BlockSpec((bq, qw), lambda h, i: (i, h)),
        compiler_params=_params(("arbitrary", "arbitrary")),
        name="attn_window",
    )(sink, p_lat, p_lat, p_lat, p_lat, p_lat, p_lat, p_lat, p_ctx, p_ctx, p_lat)


def _attn_ctx_kernel(sink_ref, q_ref, k_ref, v_ref, g_ref, o_ref, *, n_ctx):
    idx = pl.program_id(0)
    t = idx // KV_HEADS
    h = idx % KV_HEADS
    q3 = _stack_heads(q_ref)
    s = _dot_nt(q3, k_ref[...])
    sink = _sink_column(sink_ref, h * Q_PER_KV, n_ctx, t=t)
    m = jnp.maximum(jnp.max(s, axis=-1, keepdims=True), sink)
    p = jnp.exp2(s - m)
    den = jnp.sum(p, axis=-1, keepdims=True) + jnp.exp2(sink - m)
    out = _dot(p.astype(BF16), v_ref[...]) / den
    for g in range(Q_PER_KV):
        gate = g_ref[:, g * HEAD_DIM:(g + 1) * HEAD_DIM].astype(F32)
        o_ref[:, g * HEAD_DIM:(g + 1) * HEAD_DIM] = (out[g * n_ctx:(g + 1) * n_ctx] * gate).astype(BF16)


def _attn_ctx(p_ctx, sinks):
    n_ctx = p_ctx.shape[0]
    qw = Q_PER_KV * HEAD_DIM
    return pl.pallas_call(
        functools.partial(_attn_ctx_kernel, n_ctx=n_ctx),
        out_shape=jax.ShapeDtypeStruct((n_ctx, 2 * Q_HEADS * HEAD_DIM), BF16),
        grid=(2 * KV_HEADS,),
        in_specs=[
            pl.BlockSpec(memory_space=pltpu.SMEM),
            pl.BlockSpec((n_ctx, qw), lambda i: (0, i)),
            pl.BlockSpec((n_ctx, HEAD_DIM), lambda i: (0, SLOT_AK + i)),
            pl.BlockSpec((n_ctx, HEAD_DIM), lambda i: (0, SLOT_AV + i)),
            pl.BlockSpec((n_ctx, qw), lambda i: (0, SLOT_AG // Q_PER_KV + i)),
        ],
        out_specs=pl.BlockSpec((n_ctx, qw), lambda i: (0, i)),
        compiler_params=_params(("arbitrary",)),
        name="attn_ctx",
    )(sinks, p_ctx, p_ctx, p_ctx, p_ctx)


def _complex_mul(a_re, a_im, b_re, b_im):
    return a_re * b_re - a_im * b_im, a_re * b_im + a_im * b_re


def _dot_split(a, b):
    a_hi = a.astype(BF16)
    b_hi = b.astype(BF16)
    a_lo = (a - a_hi.astype(F32)).astype(BF16)
    b_lo = (b - b_hi.astype(F32)).astype(BF16)
    return _dot(a_hi, b_hi) + _dot(a_hi, b_lo) + _dot(a_lo, b_hi)


def _ssm_prep_kernel(lam_r_ref, bd_b_ref, bd_c_ref, m_ref, w_ref, v_ref, sc_ref):
    T = SSM_T
    row8 = lax.broadcasted_iota(jnp.int32, (SCAN_ROWS, 1), 0)
    n_exp = T + 1 + 3 + SCAN_ROWS
    n_col = -(-(T + 1) // SCAN_ROWS) * SCAN_ROWS
    erow = lax.broadcasted_iota(jnp.int32, (n_exp, 1), 0)
    lag_kernels = []
    for d in range(2):
        lam_re, lam_im = lam_r_ref[d, 0:1, :], lam_r_ref[d, 1:2, :]
        step = jnp.exp(lam_r_ref[d, 2:3, :])
        dist = erow - (T + 4)
        dist = dist if d == 0 else SCAN_ROWS - 1 - dist
        expo = jnp.where(erow <= T, erow,
                         jnp.where(erow == T + 1, 2 * T,
                                   jnp.where(erow == T + 2, 4 * T,
                                             jnp.where(erow == T + 3, 8 * T, T * dist)))).astype(F32)
        mag = jnp.exp(lam_re * step * expo)
        ang = lam_im * step * expo
        p_re, p_im = mag * jnp.cos(ang), mag * jnp.sin(ang)
        pc_re, pc_im = p_re[0:n_col].T, p_im[0:n_col].T

        nr, ni = p_re[1:2] - 1.0, p_im[1:2]
        den = lam_re * lam_re + lam_im * lam_im
        q_re = (nr * lam_re + ni * lam_im) / den
        q_im = (ni * lam_re - nr * lam_im) / den
        bb_re, bb_im = _complex_mul(bd_b_ref[d, 0], bd_b_ref[d, 1], q_re, q_im)
        c_re, c_im = bd_c_ref[d, 0], bd_c_ref[d, 1]

        def w_rows(e):
            g_re, g_im = _complex_mul(bb_re, bb_im, p_re[e:e + 1], p_im[e:e + 1])
            return jnp.concatenate([g_re, g_im], axis=1)

        def v_cols(e):
            g_re, g_im = _complex_mul(c_re, c_im, pc_re[:, e:e + 1], pc_im[:, e:e + 1])
            return jnp.concatenate([g_re, -g_im], axis=0)

        w = jnp.concatenate([w_rows(T - 1 - t if d == 0 else t) for t in range(T)], axis=0)
        w_ref[d] = w.astype(BF16)
        v_ref[d] = jnp.concatenate([v_cols(t + 1 if d == 0 else T - t) for t in range(T)], axis=1).astype(BF16)
        lags = _dot_split(w, v_cols(0))
        lag_of_block = [(T - 1 - t if d == 0 else t) for t in range(T)]
        lag_kernels.append({lag: lags[t * HEAD_DIM:(t + 1) * HEAD_DIM] for t, lag in enumerate(lag_of_block)})

        ahead = (lambda k: row8 >= k) if d == 0 else (lambda k: row8 <= SCAN_ROWS - 1 - k)
        rows_of = {1: T, 2: T + 1, 4: T + 2, 8: T + 3}
        parts_re, parts_im = [], []
        for k in (1, 2, 4):
            r = rows_of[k]
            parts_re.append(jnp.where(ahead(k), p_re[r:r + 1], 0.0))
            parts_im.append(jnp.where(ahead(k), p_im[r:r + 1], 0.0))
        parts_re.append(p_re[T + 4:T + 4 + SCAN_ROWS])
        parts_im.append(p_im[T + 4:T + 4 + SCAN_ROWS])
        r = rows_of[8]
        parts_re.append(jnp.broadcast_to(p_re[r:r + 1], (SCAN_ROWS, p_re.shape[1])))
        parts_im.append(jnp.broadcast_to(p_im[r:r + 1], (SCAN_ROWS, p_im.shape[1])))
        sc_ref[d, 0] = jnp.concatenate(parts_re, axis=0)
        sc_ref[d, 1] = jnp.concatenate(parts_im, axis=0)

    fwd, bwd = lag_kernels
    for t in range(T):
        blocks = []
        for t2 in range(T):
            if t2 > t:
                blocks.append(fwd[t2 - t])
            elif t2 < t:
                blocks.append(bwd[t - t2])
            else:
                blocks.append(fwd[0] + bwd[0])
        m_ref[t * HEAD_DIM:(t + 1) * HEAD_DIM, :] = jnp.concatenate(blocks, axis=1).astype(BF16)


def _ssm_prep(lam_re, lam_im, log_step, b_re, b_im, c_re, c_im):
    n_dir, groups, state = lam_re.shape
    bg = SSM_BLOCK_GROUPS
    nb = groups // bg
    ns = SSM_BLOCK_STATES
    width = SSM_T * HEAD_DIM
    step = jnp.broadcast_to(log_step[:, :, None], lam_re.shape)
    rows = jnp.stack([lam_re, lam_im, step], axis=1)
    lam_r = rows.reshape(n_dir, 3, nb, ns).transpose(2, 0, 1, 3)
    eye = jnp.eye(bg, dtype=F32)

    def b_layout(b):
        bt = b.reshape(n_dir, nb, bg, state, SSM_GROUP)
        return jnp.einsum('dbgpj,gh->bdgjhp', bt, eye).reshape(nb, n_dir, HEAD_DIM, ns)

    def c_layout(c):
        ct = c.reshape(n_dir, nb, bg, SSM_GROUP, state)
        return jnp.einsum('dbhip,gh->bdhpgi', ct, eye).reshape(nb, n_dir, ns, HEAD_DIM)

    bd_b = jnp.stack([b_layout(b_re), b_layout(b_im)], axis=2)
    bd_c = jnp.stack([c_layout(c_re), c_layout(c_im)], axis=2)
    return pl.pallas_call(
        _ssm_prep_kernel,
        out_shape=(
            jax.ShapeDtypeStruct((nb, width, width), BF16),
            jax.ShapeDtypeStruct((nb, n_dir, width, 2 * ns), BF16),
            jax.ShapeDtypeStruct((nb, n_dir, 2 * ns, width), BF16),
            jax.ShapeDtypeStruct((nb, n_dir, 2, 5 * SCAN_ROWS, ns), F32),
        ),
        grid=(nb,),
        in_specs=[
            pl.BlockSpec((None, n_dir, 3, ns), lambda b: (b, 0, 0, 0)),
            pl.BlockSpec((None, n_dir, 2, HEAD_DIM, ns), lambda b: (b, 0, 0, 0, 0)),
            pl.BlockSpec((None, n_dir, 2, ns, HEAD_DIM), lambda b: (b, 0, 0, 0, 0)),
        ],
        out_specs=(
            pl.BlockSpec((None, width, width), lambda b: (b, 0, 0)),
            pl.BlockSpec((None, n_dir, width, 2 * ns), lambda b: (b, 0, 0, 0)),
            pl.BlockSpec((None, n_dir, 2 * ns, width), lambda b: (b, 0, 0, 0)),
            pl.BlockSpec((None, n_dir, 2, 5 * SCAN_ROWS, ns), lambda b: (b, 0, 0, 0, 0)),
        ),
        compiler_params=_params(("arbitrary",)),
        name="ssm_prep",
    )(lam_r, bd_b, bd_c)


def _chunk_rows(u_ref, rb):
    return jnp.concatenate([u_ref[pl.ds(t, rb, stride=SSM_T), :] for t in range(SSM_T)], axis=1).astype(BF16)


def _ssm_states_kernel(uf_ref, ub_ref, w_ref, sc_ref, cin_ref, hf_ref, hb_ref, cout_ref, s_s, carry_s, *, rb, nrb):
    step = pl.program_id(1)
    ns = SSM_BLOCK_STATES
    row8 = lax.broadcasted_iota(jnp.int32, (SCAN_ROWS, 1), 0)

    @pl.when(step == 0)
    def _():
        carry_s[...] = cin_ref[...]

    for d, (u_ref, h_ref) in enumerate(((uf_ref, hf_ref), (ub_ref, hb_ref))):
        s_s[d] = _dot(_chunk_rows(u_ref, rb), w_ref[d])
        inner = (row8 >= 1) if d == 0 else (row8 <= SCAN_ROWS - 2)
        em = jnp.where(inner, 1.0, 0.0).astype(F32)
        consts = [(sc_ref[d, 0, i * SCAN_ROWS:(i + 1) * SCAN_ROWS, :], sc_ref[d, 1, i * SCAN_ROWS:(i + 1) * SCAN_ROWS, :])
                  for i in range(5)]
        n_groups = rb // SCAN_ROWS

        def body(g, carry, d=d, em=em, consts=consts):
            c_re, c_im = carry
            gi = g if d == 0 else n_groups - 1 - g
            rows = pl.ds(pl.multiple_of(gi * SCAN_ROWS, SCAN_ROWS), SCAN_ROWS)
            p_re = s_s[d, rows, 0:ns]
            p_im = s_s[d, rows, ns:2 * ns]
            for i, k in enumerate((1, 2, 4)):
                shift = k if d == 0 else SCAN_ROWS - k
                m_re, m_im = _complex_mul(consts[i][0], consts[i][1], pltpu.roll(p_re, shift, 0), pltpu.roll(p_im, shift, 0))
                p_re, p_im = p_re + m_re, p_im + m_im
            shift = 1 if d == 0 else SCAN_ROWS - 1
            k_re, k_im = _complex_mul(consts[3][0], consts[3][1], c_re, c_im)
            s_s[d, rows, 0:ns] = em * pltpu.roll(p_re, shift, 0) + k_re
            s_s[d, rows, ns:2 * ns] = em * pltpu.roll(p_im, shift, 0) + k_im
            last = SCAN_ROWS - 1 if d == 0 else 0
            n_re, n_im = _complex_mul(consts[4][0][0:1], consts[4][1][0:1], c_re, c_im)
            return p_re[last:last + 1] + n_re, p_im[last:last + 1] + n_im

        c_re, c_im = lax.fori_loop(0, n_groups, body, (carry_s[d, 0, 0:1, :], carry_s[d, 1, 0:1, :]), unroll=True)
        carry_s[d, 0, 0:1, :] = c_re
        carry_s[d, 1, 0:1, :] = c_im
        h_ref[...] = s_s[d].astype(BF16)

    @pl.when(step == nrb - 1)
    def _():
        cout_ref[...] = carry_s[...]


def _ssm_states(u, w, sc, carry_in, first):
    n_tok, width = u.shape
    nb = width // HEAD_DIM
    n_chunks = n_tok // SSM_T
    rb = min(SSM_ROW_BLOCK, n_chunks)
    nrb = n_chunks // rb
    ns = SSM_BLOCK_STATES
    h_shape = jax.ShapeDtypeStruct((n_chunks, nb * 2 * ns), BF16)
    return pl.pallas_call(
        functools.partial(_ssm_states_kernel, rb=rb, nrb=nrb),
        out_shape=(h_shape, h_shape, jax.ShapeDtypeStruct(carry_in.shape, F32)),
        grid=(nb, nrb),
        in_specs=[
            pl.BlockSpec((rb * SSM_T, HEAD_DIM), lambda b, r: (r, b)),
            pl.BlockSpec((rb * SSM_T, HEAD_DIM), lambda b, r: (nrb - 1 - r, b)),
            pl.BlockSpec((None, 2, SSM_T * HEAD_DIM, 2 * ns), lambda b, r: (first + b, 0, 0, 0)),
            pl.BlockSpec((None, 2, 2, 5 * SCAN_ROWS, ns), lambda b, r: (first + b, 0, 0, 0, 0)),
            pl.BlockSpec((None, 2, 2, SCAN_ROWS, ns), lambda b, r: (b, 0, 0, 0, 0)),
        ],
        out_specs=(
            pl.BlockSpec((rb, 2 * ns), lambda b, r: (r, b)),
            pl.BlockSpec((rb, 2 * ns), lambda b, r: (nrb - 1 - r, b)),
            pl.BlockSpec((None, 2, 2, SCAN_ROWS, ns), lambda b, r: (b, 0, 0, 0, 0)),
        ),
        scratch_shapes=[pltpu.VMEM((2, rb, 2 * ns), F32), pltpu.VMEM((2, 2, SCAN_ROWS, ns), F32)],
        compiler_params=_params(("arbitrary", "arbitrary")),
        name="ssm_states",
    )(u, u, w, sc, carry_in)


def _ssm_y_kernel(u_ref, hf_ref, hb_ref, m_ref, v_ref, y_ref, *, rb):
    ycat = (_dot(_chunk_rows(u_ref, rb), m_ref[...]) + _dot(hf_ref[...], v_ref[0]) + _dot(hb_ref[...], v_ref[1]))
    for t in range(SSM_T):
        y_ref[pl.ds(t, rb, stride=SSM_T), :] = ycat[:, t * HEAD_DIM:(t + 1) * HEAD_DIM]


def _ssm_y(u, hf, hb, m, v, first):
    n_tok, width = u.shape
    nb = width // HEAD_DIM
    n_chunks = n_tok // SSM_T
    rb = min(SSM_ROW_BLOCK, n_chunks)
    ns = SSM_BLOCK_STATES
    cw = SSM_T * HEAD_DIM
    return pl.pallas_call(
        functools.partial(_ssm_y_kernel, rb=rb),
        out_shape=jax.ShapeDtypeStruct((n_tok, width), F32),
        grid=(nb, n_chunks // rb),
        in_specs=[
            pl.BlockSpec((rb * SSM_T, HEAD_DIM), lambda b, r: (r, b)),
            pl.BlockSpec((rb, 2 * ns), lambda b, r: (r, b)),
            pl.BlockSpec((rb, 2 * ns), lambda b, r: (r, b)),
            pl.BlockSpec((None, cw, cw), lambda b, r: (first + b, 0, 0)),
            pl.BlockSpec((None, 2, 2 * ns, cw), lambda b, r: (first + b, 0, 0, 0)),
        ],
        out_specs=pl.BlockSpec((rb * SSM_T, HEAD_DIM), lambda b, r: (r, b)),
        compiler_params=_params(("arbitrary", "arbitrary")),
        name="ssm_y",
    )(u, hf, hb, m, v)


def _outproj_kernel(*refs, n_attn, final):
    mix = refs[:n_attn]
    y_ref, u_ref, sg_ref, d_ref, wg_ref, bg_ref, w_ref, x_ref, gate_ref = refs[n_attn:n_attn + 9]
    rest = refs[n_attn + 9:]
    y = y_ref[...] + d_ref[...] * u_ref[...].astype(F32)
    y = jax.nn.gelu(y, approximate=True)
    z = _dot(y.astype(BF16), wg_ref[...]) + bg_ref[...]
    mix_s = (y * jax.nn.sigmoid(z) * sg_ref[...].astype(F32)).astype(BF16)
    acc = None
    start = 0
    for a in [m[...] for m in mix] + [mix_s]:
        part = _dot(a, w_ref[start:start + a.shape[1], :])
        acc = part if acc is None else acc + part
        start += a.shape[1]
    x = x_ref[...] + gate_ref[...] * acc
    if final:
        fg_ref, o_ref = rest
        x = x * lax.rsqrt(jnp.mean(x * x, axis=-1, keepdims=True) + NORM_EPS) * fg_ref[...]
    else:
        (o_ref,) = rest
    o_ref[...] = x


def _outproj(attn_mixes, y, p, d_skip, w_glu, b_glu, w_out, x, gate, final_gain, bm):
    n, d = x.shape
    s_width = y.shape[1]
    blk = s_width // HEAD_DIM
    row = lambda i: (i, 0)
    const = lambda i: (0, 0)
    in_specs = [pl.BlockSpec((bm, a.shape[1]), row) for a in attn_mixes]
    in_specs += [
        pl.BlockSpec((bm, s_width), row),
        pl.BlockSpec((bm, s_width), lambda i: (i, SLOT_SU // blk)),
        pl.BlockSpec((bm, s_width), lambda i: (i, SLOT_SG // blk)),
        pl.BlockSpec((1, s_width), const),
        pl.BlockSpec((s_width, s_width), const),
        pl.BlockSpec((1, s_width), const),
        pl.BlockSpec(w_out.shape, const, pipeline_mode=pl.Buffered(1)),
        pl.BlockSpec((bm, d), row),
        pl.BlockSpec((1, d), const),
    ]
    args = list(attn_mixes) + [y, p, p, d_skip, w_glu, b_glu, w_out, x, gate]
    final = final_gain is not None
    if final:
        in_specs.append(pl.BlockSpec((1, d), const))
        args.append(final_gain)
    return pl.pallas_call(
        functools.partial(_outproj_kernel, n_attn=len(attn_mixes), final=final),
        out_shape=jax.ShapeDtypeStruct((n, d), F32),
        grid=(n // bm,),
        in_specs=in_specs,
        out_specs=pl.BlockSpec((bm, d), row),
        compiler_params=_params(("arbitrary",)),
        name="outproj_final" if final else "outproj",
    )(*args)


def _rope_tables(n):
    pos = np.arange(n)
    half = HEAD_DIM // 4
    freqs = (ROPE_THETA ** (-np.arange(half, dtype=np.float32) / half)).astype(np.float32)
    ang_r = ((pos // GRID_W).astype(np.float32)[:, None] * freqs[None, :]).astype(np.float64)
    ang_c = ((pos % GRID_W).astype(np.float32)[:, None] * freqs[None, :]).astype(np.float64)
    cos = np.concatenate([np.cos(ang_r)] * 2 + [np.cos(ang_c)] * 2, axis=-1)
    sin = np.concatenate([-np.sin(ang_r), np.sin(ang_r), -np.sin(ang_c), np.sin(ang_c)], axis=-1)
    return jnp.asarray(cos, F32), jnp.asarray(sin, F32)


def _reorder_w_in(w):
    hd = HEAD_DIM
    sizes = (6 * hd, 2 * hd, 2 * hd, 6 * hd, 6 * hd, 2 * hd, 2 * hd, 6 * hd, 4 * hd, 4 * hd)
    names = ("aq", "ak", "av", "ag", "bq", "bk", "bv", "bg", "su", "sg")
    parts, start = {}, 0
    for name, size in zip(names, sizes):
        parts[name] = w[:, start:start + size]
        start += size
    order = ("aq", "bq", "ak", "bk", "av", "bv", "su", "ag", "bg", "sg")
    return jnp.concatenate([parts[k] for k in order], axis=1).astype(BF16)


def kernel(x, c, ctx, c_ctx, w_ada, b_ada, norm_gain, w_in, a_q_gain, a_k_gain, b_sink, ssm_lambda_re,
           ssm_lambda_im, ssm_log_step, ssm_b_re, ssm_b_im, ssm_c_re, ssm_c_im, ssm_d, w_glu, b_glu, w_out,
           final_gain):
    depth, d = norm_gain.shape
    n = x.shape[1]
    n_ctx = ctx.shape[1]
    assert x.shape[0] == 1 and n % 512 == 0 and n_ctx % (SSM_T * SCAN_ROWS) == 0
    xs = x[0]
    cs = ctx[0]
    bm = 512

    c_t = jnp.concatenate([c.reshape(d, 1), c_ctx.reshape(d, 1)], axis=1)
    mod = _modulation(c_t, w_ada, b_ada)
    tables = _rope_tables(n)

    fold = lambda a: jnp.moveaxis(a, 0, 1).reshape((a.shape[1], depth * a.shape[2]) + a.shape[3:])
    m_op, w_op, v_op, scan_c = _ssm_prep(fold(ssm_lambda_re), fold(ssm_lambda_im), fold(ssm_log_step),
                                        fold(ssm_b_re), fold(ssm_b_im), fold(ssm_c_re), fold(ssm_c_im))
    blocks_per_layer = ssm_lambda_re.shape[2] // SSM_BLOCK_GROUPS

    for layer in range(depth):
        last = layer == depth - 1
        shift, scale, gate = (mod[layer, :, i * d:(i + 1) * d] for i in range(3))
        w = _reorder_w_in(w_in[layer])
        gain = norm_gain[layer].reshape(1, d)
        qk_gain = jnp.stack([a_q_gain[layer], a_k_gain[layer]], axis=0)
        p_lat, qt_lat, vt_lat, su_lat = _inproj(xs, shift[0:1], scale[0:1], gain, w, qk_gain, tables, bm)
        p_ctx, _, vt_ctx, su_ctx = _inproj(cs, shift[1:2], scale[1:2], gain, w, qk_gain, None, n_ctx)

        mix_a = _attn_global(p_lat, p_ctx, qt_lat, vt_lat, vt_ctx, bq=_largest_block(n, 2048),
                             bk=_largest_block(n + n_ctx, 1280))
        mix_b = _attn_window(p_lat, p_ctx, b_sink[layer], bq=_largest_block(n, 1024))

        first = layer * blocks_per_layer
        no_state = jnp.zeros((blocks_per_layer, 2, 2, SCAN_ROWS, SSM_BLOCK_STATES), F32)
        hf_ctx, hb_ctx, ctx_state = _ssm_states(su_ctx, w_op, scan_c, no_state, first)
        hf_lat, hb_lat, _ = _ssm_states(su_lat, w_op, scan_c, ctx_state, first)
        y_lat = _ssm_y(su_lat, hf_lat, hb_lat, m_op, v_op, first)
        d_skip = ssm_d[layer].reshape(1, -1)
        wg = w_glu[layer].astype(BF16)
        bg = b_glu[layer].reshape(1, -1)
        wo = w_out[layer].astype(BF16)
        xs_new = _outproj((mix_a, mix_b), y_lat, p_lat, d_skip, wg, bg, wo, xs, gate[0:1],
                          final_gain.reshape(1, d) if last else None, bm)
        if not last:
            sinks = jnp.stack([jnp.full_like(b_sink[layer], NEG_INF), b_sink[layer]], axis=0)
            mix_ab_c = _attn_ctx(p_ctx, sinks)
            y_ctx = _ssm_y(su_ctx, hf_ctx, hb_ctx, m_op, v_op, first)
            cs = _outproj((mix_ab_c,), y_ctx, p_ctx, d_skip, wg, bg, wo, cs, gate[1:2], None, n_ctx)
        xs = xs_new
    return xs[None]
```

```python
import functools
import math

import jax
import jax.numpy as jnp
import numpy as np
from jax import lax
from jax.experimental import pallas as pl
from jax.experimental.pallas import tpu as pltpu

F32 = jnp.float32
BF16 = jnp.bfloat16

HEAD_DIM = 128
GRID_W = 64
Q_PER_KV = 3
KV_HEADS = 2
Q_HEADS = Q_PER_KV * KV_HEADS
WINDOW = 128
SSM_GROUP = 16
SSM_STATE = 64
SSM_T = 8
SSM_BLOCK_GROUPS = HEAD_DIM // SSM_GROUP
SSM_BLOCK_STATES = SSM_BLOCK_GROUPS * SSM_STATE
SCAN_ROWS = 8
SSM_ROW_BLOCK = 512
ONES_ROWS = 16
ROPE_THETA = 10000.0
NORM_EPS = 1e-6
NEG_INF = -1e30
LOG2_E = math.log2(math.e)
VMEM_LIMIT_V7X = 56 * 1024 * 1024

SLOT_AQ, SLOT_BQ, SLOT_AK, SLOT_BK, SLOT_AV, SLOT_BV = 0, 6, 12, 14, 16, 18
SLOT_SU, SLOT_AG, SLOT_BG, SLOT_SG = 20, 24, 30, 36
N_SLOTS = 40
PANEL_SLOTS = 8


def _params(sem, vmem=VMEM_LIMIT_V7X):
    return pltpu.CompilerParams(dimension_semantics=sem, vmem_limit_bytes=vmem)


def _largest_block(total, limit):
    return max(b for b in range(HEAD_DIM, limit + 1, HEAD_DIM) if total % b == 0)


def _dot(a, b):
    return jnp.dot(a, b, preferred_element_type=F32)


def _dot_nt(a, b):
    return lax.dot_general(a, b, (((1,), (1,)), ((), ())), preferred_element_type=F32)


def _silu(x):
    return x * jax.nn.sigmoid(x)


def _mod_kernel(ct_ref, w_ref, b_ref, o_ref):
    s = _silu(ct_ref[...])
    w = w_ref[...]
    b = b_ref[...]
    o_ref[0:1, :] = jnp.sum(s[:, 0:1] * w, axis=0, keepdims=True) + b
    o_ref[1:2, :] = jnp.sum(s[:, 1:2] * w, axis=0, keepdims=True) + b


def _modulation(c_t, w_ada, b_ada):
    depth, d, n3 = w_ada.shape
    tn = 1024
    return pl.pallas_call(
        _mod_kernel,
        out_shape=jax.ShapeDtypeStruct((depth, 2, n3), F32),
        grid=(depth, n3 // tn),
        in_specs=[
            pl.BlockSpec((d, 2), lambda l, j: (0, 0)),
            pl.BlockSpec((None, d, tn), lambda l, j: (l, 0, j)),
            pl.BlockSpec((None, 1, tn), lambda l, j: (l, 0, j)),
        ],
        out_specs=pl.BlockSpec((None, 2, tn), lambda l, j: (l, 0, j)),
        compiler_params=_params(("arbitrary", "arbitrary")),
        name="adaln_modulation",
    )(c_t, w_ada, b_ada.reshape(depth, 1, n3))


def _slot_config(slot):
    scale = HEAD_DIM ** -0.5 * LOG2_E
    if slot < SLOT_BQ:
        return 0, True, scale, False
    if slot < SLOT_AK:
        return None, True, scale, False
    if slot < SLOT_BK:
        return 1, True, None, False
    if slot < SLOT_AV:
        return None, True, None, False
    if slot < SLOT_AG:
        return None, False, None, False
    return None, False, None, True


def _inproj_kernel(*refs, rope):
    if rope:
        x_ref, shift_ref, scale_ref, gain_ref, w_ref, qk_ref, cos_ref, sin_ref, o_ref, qt_ref, vt_ref, su_ref = refs
    else:
        x_ref, shift_ref, scale_ref, gain_ref, w_ref, qk_ref, o_ref, qt_ref, vt_ref, su_ref = refs
    x = x_ref[...]
    y = x * lax.rsqrt(jnp.mean(x * x, axis=-1, keepdims=True) + NORM_EPS) * gain_ref[...]
    h = (y * (1.0 + scale_ref[...]) + shift_ref[...]).astype(BF16)
    lane = lax.broadcasted_iota(jnp.int32, (1, HEAD_DIM), 1)
    first_half = (lane & 63) < 32
    bn = PANEL_SLOTS * HEAD_DIM
    vt_rows = HEAD_DIM + ONES_ROWS

    for panel in range(N_SLOTS // PANEL_SLOTS):
        acc = _dot(h, w_ref[:, panel * bn:(panel + 1) * bn])
        for k in range(PANEL_SLOTS):
            slot = panel * PANEL_SLOTS + k
            gain_row, rotary, scale, act = _slot_config(slot)
            t = acc[:, k * HEAD_DIM:(k + 1) * HEAD_DIM]
            if gain_row is not None:
                t = (t * lax.rsqrt(jnp.mean(t * t, axis=-1, keepdims=True) + NORM_EPS)
                     * qk_ref[gain_row:gain_row + 1, :])
            if rotary and rope:
                partner = jnp.where(first_half, pltpu.roll(t, 96, 1), pltpu.roll(t, 32, 1))
                t = t * cos_ref[...] + partner * sin_ref[...]
            if scale is not None:
                t = t * scale
            if act:
                t = _silu(t)
            o_ref[:, slot * HEAD_DIM:(slot + 1) * HEAD_DIM] = t.astype(BF16)
            if SLOT_AQ <= slot < SLOT_BQ:
                qt_ref[(slot - SLOT_AQ) * HEAD_DIM:(slot - SLOT_AQ + 1) * HEAD_DIM, :] = t.T.astype(BF16)
            if SLOT_AV <= slot < SLOT_BV:
                base = (slot - SLOT_AV) * vt_rows
                vt_ref[base:base + HEAD_DIM, :] = t.T.astype(BF16)
                vt_ref[base + HEAD_DIM:base + vt_rows, :] = jnp.ones((ONES_ROWS, t.shape[0]), BF16)
            if SLOT_SU <= slot < SLOT_AG:
                su_ref[:, (slot - SLOT_SU) * HEAD_DIM:(slot - SLOT_SU + 1) * HEAD_DIM] = t


def _inproj(x, shift, scale, gain, w, qk_gain, rope_tables, bm):
    n, d = x.shape
    width = w.shape[1]
    rope = rope_tables is not None
    row = lambda i: (i, 0)
    col = lambda i: (0, i)
    const = lambda i: (0, 0)
    in_specs = [
        pl.BlockSpec((bm, d), row),
        pl.BlockSpec((1, d), const),
        pl.BlockSpec((1, d), const),
        pl.BlockSpec((1, d), const),
        pl.BlockSpec((d, width), const, pipeline_mode=pl.Buffered(1)),
        pl.BlockSpec((2, HEAD_DIM), const),
    ]
    args = [x, shift, scale, gain, w, qk_gain]
    if rope:
        in_specs += [pl.BlockSpec((bm, HEAD_DIM), row), pl.BlockSpec((bm, HEAD_DIM), row)]
        args += list(rope_tables)
    qt_rows = Q_HEADS * HEAD_DIM
    vt_rows = KV_HEADS * (HEAD_DIM + ONES_ROWS)
    su_width = (SLOT_AG - SLOT_SU) * HEAD_DIM
    return pl.pallas_call(
        functools.partial(_inproj_kernel, rope=rope),
        out_shape=(
            jax.ShapeDtypeStruct((n, width), BF16),
            jax.ShapeDtypeStruct((qt_rows, n), BF16),
            jax.ShapeDtypeStruct((vt_rows, n), BF16),
            jax.ShapeDtypeStruct((n, su_width), F32),
        ),
        grid=(n // bm,),
        in_specs=in_specs,
        out_specs=(
            pl.BlockSpec((bm, width), row),
            pl.BlockSpec((qt_rows, bm), col),
            pl.BlockSpec((vt_rows, bm), col),
            pl.BlockSpec((bm, su_width), row),
        ),
        compiler_params=_params(("arbitrary",)),
        name="inproj_rope" if rope else "inproj_ctx",
    )(*args)


def _stack_heads(ref, rows=None):
    sl = slice(None) if rows is None else rows
    return jnp.concatenate([ref[sl, g * HEAD_DIM:(g + 1) * HEAD_DIM] for g in range(Q_PER_KV)], axis=0)


def _attn_global_kernel(qt_ref, qn_ref, k0_ref, kn_ref, vt_ref, g_ref, o_ref, qa_s, s_s, mb_s, m_s, acc_s,
                        *, bq, nk, tq):
    qi = pl.program_id(1)
    ki = pl.program_id(2)
    tiles = [(g, c) for g in range(Q_PER_KV) for c in range(bq // tq)]

    def lanes(g, c):
        return slice(g * bq + c * tq, g * bq + (c + 1) * tq)

    def scores(k_ref, g, c):
        s = _dot(k_ref[...], qa_s[g * HEAD_DIM:(g + 1) * HEAD_DIM, c * tq:(c + 1) * tq])
        s_s[:, lanes(g, c)] = s
        mb_s[:, lanes(g, c)] = jnp.max(s, axis=0, keepdims=True)

    @pl.when((qi == 0) & (ki == 0))
    def _():
        qa_s[...] = qt_ref[...]
        for g, c in tiles:
            scores(k0_ref, g, c)

    @pl.when(ki == 0)
    def _():
        m_s[...] = jnp.full(m_s.shape, NEG_INF, F32)
        acc_s[...] = jnp.zeros(acc_s.shape, F32)

    @pl.when(ki == nk - 1)
    def _():
        qa_s[...] = qn_ref[...]

    vt = vt_ref[...]
    for g, c in tiles:
        cols = lanes(g, c)
        m_prev = m_s[:, cols]
        m_new = jnp.maximum(m_prev, mb_s[:, cols])
        alpha = jnp.exp2(m_prev - m_new)
        p = jnp.exp2((s_s[:, cols] - m_new).astype(BF16))
        acc_s[:, cols] = alpha * acc_s[:, cols] + _dot(vt, p)
        m_s[:, cols] = m_new
        scores(kn_ref, g, c)

    @pl.when(ki == nk - 1)
    def _():
        for g in range(Q_PER_KV):
            cols = slice(g * bq, (g + 1) * bq)
            out_t = acc_s[0:HEAD_DIM, cols] / acc_s[HEAD_DIM:HEAD_DIM + 1, cols]
            gate = g_ref[:, g * HEAD_DIM:(g + 1) * HEAD_DIM].astype(F32)
            o_ref[:, g * HEAD_DIM:(g + 1) * HEAD_DIM] = (out_t.T * gate).astype(BF16)


def _attn_global(p_lat, p_ctx, qt, vt_lat, vt_ctx, bq, bk):
    n = p_lat.shape[0]
    hd = HEAD_DIM
    qw = Q_PER_KV * hd
    ones_rows = ONES_ROWS
    k_all = jnp.concatenate([p_lat[:, SLOT_AK * hd:SLOT_BK * hd], p_ctx[:, SLOT_AK * hd:SLOT_BK * hd]], axis=0)
    vt_ext = jnp.concatenate([vt_lat, vt_ctx], axis=1)
    n_kv = k_all.shape[0]
    assert n_kv % bk == 0 and n % bq == 0
    nk = n_kv // bk
    nq = n // bq
    width = Q_PER_KV * bq
    return pl.pallas_call(
        functools.partial(_attn_global_kernel, bq=bq, nk=nk, tq=512),
        out_shape=jax.ShapeDtypeStruct((n, Q_HEADS * hd), BF16),
        grid=(KV_HEADS, nq, nk),
        in_specs=[
            pl.BlockSpec((qw, bq), lambda h, i, k: (h, i)),
            pl.BlockSpec((qw, bq), lambda h, i, k: (h, jnp.minimum(i + 1, nq - 1))),
            pl.BlockSpec((bk, hd), lambda h, i, k: (0, h)),
            pl.BlockSpec((bk, hd), lambda h, i, k: ((k + 1) % nk, h)),
            pl.BlockSpec((hd + ones_rows, bk), lambda h, i, k: (h, k)),
            pl.BlockSpec((bq, qw), lambda h, i, k: (i, SLOT_AG // Q_PER_KV + h)),
        ],
        out_specs=pl.BlockSpec((bq, qw), lambda h, i, k: (i, h)),
        scratch_shapes=[
            pltpu.VMEM((qw, bq), BF16),
            pltpu.VMEM((bk, width), F32),
            pltpu.VMEM((1, width), F32),
            pltpu.VMEM((1, width), F32),
            pltpu.VMEM((hd + ones_rows, width), F32),
        ],
        compiler_params=_params(("arbitrary", "arbitrary", "arbitrary")),
        name="attn_global",
    )(qt, qt, k_all, k_all, vt_ext, p_lat)


def _sink_column(sink_ref, base, rows_per_head, t=None):
    row = lax.broadcasted_iota(jnp.int32, (Q_PER_KV * rows_per_head, 1), 0)
    get = (lambda g: sink_ref[base + g]) if t is None else (lambda g: sink_ref[t, base + g])
    get = functools.partial(lambda f, g: f(g) * LOG2_E, get)
    return jnp.where(row < rows_per_head, get(0), jnp.where(row < 2 * rows_per_head, get(1), get(2)))


def _attn_window_kernel(sink_ref, q_ref, kp_ref, km_ref, kn_ref, vp_ref, vm_ref, vn_ref, kc_ref, vc_ref,
                        g_ref, o_ref, *, bq, n):
    h = pl.program_id(0)
    qi = pl.program_id(1)
    kcat = jnp.concatenate([kp_ref[...], km_ref[...], kn_ref[...]], axis=0)
    vcat = jnp.concatenate([vp_ref[...], vm_ref[...], vn_ref[...]], axis=0)
    kc = kc_ref[...]
    vc = vc_ref[...]
    span = 3 * WINDOW
    rows = Q_PER_KV * WINDOW
    r = lax.broadcasted_iota(jnp.int32, (rows, span), 0) & (WINDOW - 1)
    cidx = lax.broadcasted_iota(jnp.int32, (rows, span), 1)
    rel = cidx - r
    band = (rel >= 0) & (rel <= 2 * WINDOW)
    sink = _sink_column(sink_ref, h * Q_PER_KV, WINDOW)
    n_sub = bq // WINDOW
    q3s = [_stack_heads(q_ref, slice(sb * WINDOW, (sb + 1) * WINDOW)) for sb in range(n_sub)]
    band_scores = [_dot_nt(q3s[sb], kcat[sb * WINDOW:sb * WINDOW + span]) for sb in range(n_sub)]
    ctx_scores = [_dot_nt(q3s[sb], kc) for sb in range(n_sub)]
    for sb in range(n_sub):
        vw = vcat[sb * WINDOW:sb * WINDOW + span]
        key_pos = qi * bq + (sb - 1) * WINDOW + cidx
        valid = band & (key_pos >= 0) & (key_pos < n)
        s = jnp.where(valid, band_scores[sb], NEG_INF)
        sc = ctx_scores[sb]
        m = jnp.maximum(jnp.maximum(jnp.max(s, axis=-1, keepdims=True), jnp.max(sc, axis=-1, keepdims=True)), sink)
        p = jnp.exp2(s - m)
        pc = jnp.exp2(sc - m)
        den = jnp.sum(p, axis=-1, keepdims=True) + jnp.sum(pc, axis=-1, keepdims=True) + jnp.exp2(sink - m)
        out = (_dot(p.astype(BF16), vw) + _dot(pc.astype(BF16), vc)) / den
        for g in range(Q_PER_KV):
            gate = g_ref[sb * WINDOW:(sb + 1) * WINDOW, g * HEAD_DIM:(g + 1) * HEAD_DIM].astype(F32)
            o_ref[sb * WINDOW:(sb + 1) * WINDOW, g * HEAD_DIM:(g + 1) * HEAD_DIM] = (
                out[g * WINDOW:(g + 1) * WINDOW] * gate).astype(BF16)


def _attn_window(p_lat, p_ctx, sink, bq):
    n = p_lat.shape[0]
    n_ctx = p_ctx.shape[0]
    qw = Q_PER_KV * HEAD_DIM
    per = bq // WINDOW
    last = n // WINDOW - 1
    prev_map = lambda slot: (lambda h, i: (jnp.maximum(i * per - 1, 0), slot + h))
    main_map = lambda slot: (lambda h, i: (i, slot + h))
    next_map = lambda slot: (lambda h, i: (jnp.minimum((i + 1) * per, last), slot + h))
    return pl.pallas_call(
        functools.partial(_attn_window_kernel, bq=bq, n=n),
        out_shape=jax.ShapeDtypeStruct((n, Q_HEADS * HEAD_DIM), BF16),
        grid=(KV_HEADS, n // bq),
        in_specs=[
            pl.BlockSpec(memory_space=pltpu.SMEM),
            pl.BlockSpec((bq, qw), lambda h, i: (i, SLOT_BQ // Q_PER_KV + h)),
            pl.BlockSpec((WINDOW, HEAD_DIM), prev_map(SLOT_BK)),
            pl.BlockSpec((bq, HEAD_DIM), main_map(SLOT_BK)),
            pl.BlockSpec((WINDOW, HEAD_DIM), next_map(SLOT_BK)),
            pl.BlockSpec((WINDOW, HEAD_DIM), prev_map(SLOT_BV)),
            pl.BlockSpec((bq, HEAD_DIM), main_map(SLOT_BV)),
            pl.BlockSpec((WINDOW, HEAD_DIM), next_map(SLOT_BV)),
            pl.BlockSpec((n_ctx, HEAD_DIM), lambda h, i: (0, SLOT_BK + h)),
            pl.BlockSpec((n_ctx, HEAD_DIM), lambda h, i: (0, SLOT_BV + h)),
            pl.BlockSpec((bq, qw), lambda h, i: (i, SLOT_BG // Q_PER_KV + h)),
        ],
        out_specs=pl.BlockSpec((bq, qw), lambda h, i: (i, h)),
        compiler_params=_params(("arbitrary", "arbitrary")),
        name="attn_window",
    )(sink, p_lat, p_lat, p_lat, p_lat, p_lat, p_lat, p_lat, p_ctx, p_ctx, p_lat)


def _attn_ctx_kernel(sink_ref, q_ref, k_ref, v_ref, g_ref, o_ref, *, n_ctx):
    idx = pl.program_id(0)
    t = idx // KV_HEADS
    h = idx % KV_HEADS
    q3 = _stack_heads(q_ref)
    s = _dot_nt(q3, k_ref[...])
    sink = _sink_column(sink_ref, h * Q_PER_KV, n_ctx, t=t)
    m = jnp.maximum(jnp.max(s, axis=-1, keepdims=True), sink)
    p = jnp.exp2(s - m)
    den = jnp.sum(p, axis=-1, keepdims=True) + jnp.exp2(sink - m)
    out = _dot(p.astype(BF16), v_ref[...]) / den
    for g in range(Q_PER_KV):
        gate = g_ref[:, g * HEAD_DIM:(g + 1) * HEAD_DIM].astype(F32)
        o_ref[:, g * HEAD_DIM:(g + 1) * HEAD_DIM] = (out[g * n_ctx:(g + 1) * n_ctx] * gate).astype(BF16)


def _attn_ctx(p_ctx, sinks):
    n_ctx = p_ctx.shape[0]
    qw = Q_PER_KV * HEAD_DIM
    return pl.pallas_call(
        functools.partial(_attn_ctx_kernel, n_ctx=n_ctx),
        out_shape=jax.ShapeDtypeStruct((n_ctx, 2 * Q_HEADS * HEAD_DIM), BF16),
        grid=(2 * KV_HEADS,),
        in_specs=[
            pl.BlockSpec(memory_space=pltpu.SMEM),
            pl.BlockSpec((n_ctx, qw), lambda i: (0, i)),
            pl.BlockSpec((n_ctx, HEAD_DIM), lambda i: (0, SLOT_AK + i)),
            pl.BlockSpec((n_ctx, HEAD_DIM), lambda i: (0, SLOT_AV + i)),
            pl.BlockSpec((n_ctx, qw), lambda i: (0, SLOT_AG // Q_PER_KV + i)),
        ],
        out_specs=pl.BlockSpec((n_ctx, qw), lambda i: (0, i)),
        compiler_params=_params(("arbitrary",)),
        name="attn_ctx",
    )(sinks, p_ctx, p_ctx, p_ctx, p_ctx)


def _complex_mul(a_re, a_im, b_re, b_im):
    return a_re * b_re - a_im * b_im, a_re * b_im + a_im * b_re


def _dot_split(a, b):
    a_hi = a.astype(BF16)
    b_hi = b.astype(BF16)
    a_lo = (a - a_hi.astype(F32)).astype(BF16)
    b_lo = (b - b_hi.astype(F32)).astype(BF16)
    return _dot(a_hi, b_hi) + _dot(a_hi, b_lo) + _dot(a_lo, b_hi)


def _ssm_prep_kernel(lam_r_ref, bd_b_ref, bd_c_ref, m_ref, w_ref, v_ref, sc_ref):
    T = SSM_T
    row8 = lax.broadcasted_iota(jnp.int32, (SCAN_ROWS, 1), 0)
    n_exp = T + 1 + 3 + SCAN_ROWS
    n_col = -(-(T + 1) // SCAN_ROWS) * SCAN_ROWS
    erow = lax.broadcasted_iota(jnp.int32, (n_exp, 1), 0)
    lag_kernels = []
    for d in range(2):
        lam_re, lam_im = lam_r_ref[d, 0:1, :], lam_r_ref[d, 1:2, :]
        step = jnp.exp(lam_r_ref[d, 2:3, :])
        dist = erow - (T + 4)
        dist = dist if d == 0 else SCAN_ROWS - 1 - dist
        expo = jnp.where(erow <= T, erow,
                         jnp.where(erow == T + 1, 2 * T,
                                   jnp.where(erow == T + 2, 4 * T,
                                             jnp.where(erow == T + 3, 8 * T, T * dist)))).astype(F32)
        mag = jnp.exp(lam_re * step * expo)
        ang = lam_im * step * expo
        p_re, p_im = mag * jnp.cos(ang), mag * jnp.sin(ang)
        pc_re, pc_im = p_re[0:n_col].T, p_im[0:n_col].T

        nr, ni = p_re[1:2] - 1.0, p_im[1:2]
        den = lam_re * lam_re + lam_im * lam_im
        q_re = (nr * lam_re + ni * lam_im) / den
        q_im = (ni * lam_re - nr * lam_im) / den
        bb_re, bb_im = _complex_mul(bd_b_ref[d, 0], bd_b_ref[d, 1], q_re, q_im)
        c_re, c_im = bd_c_ref[d, 0], bd_c_ref[d, 1]

        def w_rows(e):
            g_re, g_im = _complex_mul(bb_re, bb_im, p_re[e:e + 1], p_im[e:e + 1])
            return jnp.concatenate([g_re, g_im], axis=1)

        def v_cols(e):
            g_re, g_im = _complex_mul(c_re, c_im, pc_re[:, e:e + 1], pc_im[:, e:e + 1])
            return jnp.concatenate([g_re, -g_im], axis=0)

        w = jnp.concatenate([w_rows(T - 1 - t if d == 0 else t) for t in range(T)], axis=0)
        w_ref[d] = w.astype(BF16)
        v_ref[d] = jnp.concatenate([v_cols(t + 1 if d == 0 else T - t) for t in range(T)], axis=1).astype(BF16)
        lags = _dot_split(w, v_cols(0))
        lag_of_block = [(T - 1 - t if d == 0 else t) for t in range(T)]
        lag_kernels.append({lag: lags[t * HEAD_DIM:(t + 1) * HEAD_DIM] for t, lag in enumerate(lag_of_block)})

        ahead = (lambda k: row8 >= k) if d == 0 else (lambda k: row8 <= SCAN_ROWS - 1 - k)
        rows_of = {1: T, 2: T + 1, 4: T + 2, 8: T + 3}
        parts_re, parts_im = [], []
        for k in (1, 2, 4):
            r = rows_of[k]
            parts_re.append(jnp.where(ahead(k), p_re[r:r + 1], 0.0))
            parts_im.append(jnp.where(ahead(k), p_im[r:r + 1], 0.0))
        parts_re.append(p_re[T + 4:T + 4 + SCAN_ROWS])
        parts_im.append(p_im[T + 4:T + 4 + SCAN_ROWS])
        r = rows_of[8]
        parts_re.append(jnp.broadcast_to(p_re[r:r + 1], (SCAN_ROWS, p_re.shape[1])))
        parts_im.append(jnp.broadcast_to(p_im[r:r + 1], (SCAN_ROWS, p_im.shape[1])))
        sc_ref[d, 0] = jnp.concatenate(parts_re, axis=0)
        sc_ref[d, 1] = jnp.concatenate(parts_im, axis=0)

    fwd, bwd = lag_kernels
    for t in range(T):
        blocks = []
        for t2 in range(T):
            if t2 > t:
                blocks.append(fwd[t2 - t])
            elif t2 < t:
                blocks.append(bwd[t - t2])
            else:
                blocks.append(fwd[0] + bwd[0])
        m_ref[t * HEAD_DIM:(t + 1) * HEAD_DIM, :] = jnp.concatenate(blocks, axis=1).astype(BF16)


def _ssm_prep(lam_re, lam_im, log_step, b_re, b_im, c_re, c_im):
    n_dir, groups, state = lam_re.shape
    bg = SSM_BLOCK_GROUPS
    nb = groups // bg
    ns = SSM_BLOCK_STATES
    width = SSM_T * HEAD_DIM
    step = jnp.broadcast_to(log_step[:, :, None], lam_re.shape)
    rows = jnp.stack([lam_re, lam_im, step], axis=1)
    lam_r = rows.reshape(n_dir, 3, nb, ns).transpose(2, 0, 1, 3)
    eye = jnp.eye(bg, dtype=F32)

    def b_layout(b):
        bt = b.reshape(n_dir, nb, bg, state, SSM_GROUP)
        return jnp.einsum('dbgpj,gh->bdgjhp', bt, eye).reshape(nb, n_dir, HEAD_DIM, ns)

    def c_layout(c):
        ct = c.reshape(n_dir, nb, bg, SSM_GROUP, state)
        return jnp.einsum('dbhip,gh->bdhpgi', ct, eye).reshape(nb, n_dir, ns, HEAD_DIM)

    bd_b = jnp.stack([b_layout(b_re), b_layout(b_im)], axis=2)
    bd_c = jnp.stack([c_layout(c_re), c_layout(c_im)], axis=2)
    return pl.pallas_call(
        _ssm_prep_kernel,
        out_shape=(
            jax.ShapeDtypeStruct((nb, width, width), BF16),
            jax.ShapeDtypeStruct((nb, n_dir, width, 2 * ns), BF16),
            jax.ShapeDtypeStruct((nb, n_dir, 2 * ns, width), BF16),
            jax.ShapeDtypeStruct((nb, n_dir, 2, 5 * SCAN_ROWS, ns), F32),
        ),
        grid=(nb,),
        in_specs=[
            pl.BlockSpec((None, n_dir, 3, ns), lambda b: (b, 0, 0, 0)),
            pl.BlockSpec((None, n_dir, 2, HEAD_DIM, ns), lambda b: (b, 0, 0, 0, 0)),
            pl.BlockSpec((None, n_dir, 2, ns, HEAD_DIM), lambda b: (b, 0, 0, 0, 0)),
        ],
        out_specs=(
            pl.BlockSpec((None, width, width), lambda b: (b, 0, 0)),
            pl.BlockSpec((None, n_dir, width, 2 * ns), lambda b: (b, 0, 0, 0)),
            pl.BlockSpec((None, n_dir, 2 * ns, width), lambda b: (b, 0, 0, 0)),
            pl.BlockSpec((None, n_dir, 2, 5 * SCAN_ROWS, ns), lambda b: (b, 0, 0, 0, 0)),
        ),
        compiler_params=_params(("arbitrary",)),
        name="ssm_prep",
    )(lam_r, bd_b, bd_c)


def _chunk_rows(u_ref, rb):
    return jnp.concatenate([u_ref[pl.ds(t, rb, stride=SSM_T), :] for t in range(SSM_T)], axis=1).astype(BF16)


def _ssm_states_kernel(uf_ref, ub_ref, w_ref, sc_ref, cin_ref, hf_ref, hb_ref, cout_ref, s_s, carry_s, *, rb, nrb):
    step = pl.program_id(1)
    ns = SSM_BLOCK_STATES
    row8 = lax.broadcasted_iota(jnp.int32, (SCAN_ROWS, 1), 0)

    @pl.when(step == 0)
    def _():
        carry_s[...] = cin_ref[...]

    for d, (u_ref, h_ref) in enumerate(((uf_ref, hf_ref), (ub_ref, hb_ref))):
        s_s[d] = _dot(_chunk_rows(u_ref, rb), w_ref[d])
        first, last = (0, SCAN_ROWS - 1) if d == 0 else (SCAN_ROWS - 1, 0)
        first_row = row8 == first
        consts = [(sc_ref[d, 0, i * SCAN_ROWS:(i + 1) * SCAN_ROWS, :], sc_ref[d, 1, i * SCAN_ROWS:(i + 1) * SCAN_ROWS, :])
                  for i in range(3)]
        a_re, a_im = sc_ref[d, 0, last:last + 1, :], sc_ref[d, 1, last:last + 1, :]
        n_groups = rb // SCAN_ROWS

        def body(g, carry, d=d, first_row=first_row, last=last, consts=consts, a_re=a_re, a_im=a_im):
            c_re, c_im = carry
            gi = g if d == 0 else n_groups - 1 - g
            rows = pl.ds(pl.multiple_of(gi * SCAN_ROWS, SCAN_ROWS), SCAN_ROWS)
            ac_re, ac_im = _complex_mul(a_re, a_im, c_re, c_im)
            p_re = s_s[d, rows, 0:ns] + jnp.where(first_row, ac_re, 0.0)
            p_im = s_s[d, rows, ns:2 * ns] + jnp.where(first_row, ac_im, 0.0)
            for i, k in enumerate((1, 2, 4)):
                shift = k if d == 0 else SCAN_ROWS - k
                m_re, m_im = _complex_mul(consts[i][0], consts[i][1], pltpu.roll(p_re, shift, 0), pltpu.roll(p_im, shift, 0))
                p_re, p_im = p_re + m_re, p_im + m_im
            shift = 1 if d == 0 else SCAN_ROWS - 1
            s_s[d, rows, 0:ns] = jnp.where(first_row, c_re, pltpu.roll(p_re, shift, 0))
            s_s[d, rows, ns:2 * ns] = jnp.where(first_row, c_im, pltpu.roll(p_im, shift, 0))
            return p_re[last:last + 1], p_im[last:last + 1]

        c_re, c_im = lax.fori_loop(0, n_groups, body, (carry_s[d, 0, 0:1, :], carry_s[d, 1, 0:1, :]), unroll=True)
        carry_s[d, 0, 0:1, :] = c_re
        carry_s[d, 1, 0:1, :] = c_im
        h_ref[...] = s_s[d].astype(BF16)

    @pl.when(step == nrb - 1)
    def _():
        cout_ref[...] = carry_s[...]


def _ssm_states(u, w, sc, carry_in, first):
    n_tok, width = u.shape
    nb = width // HEAD_DIM
    n_chunks = n_tok // SSM_T
    rb = min(SSM_ROW_BLOCK, n_chunks)
    nrb = n_chunks // rb
    ns = SSM_BLOCK_STATES
    h_shape = jax.ShapeDtypeStruct((n_chunks, nb * 2 * ns), BF16)
    return pl.pallas_call(
        functools.partial(_ssm_states_kernel, rb=rb, nrb=nrb),
        out_shape=(h_shape, h_shape, jax.ShapeDtypeStruct(carry_in.shape, F32)),
        grid=(nb, nrb),
        in_specs=[
            pl.BlockSpec((rb * SSM_T, HEAD_DIM), lambda b, r: (r, b)),
            pl.BlockSpec((rb * SSM_T, HEAD_DIM), lambda b, r: (nrb - 1 - r, b)),
            pl.BlockSpec((None, 2, SSM_T * HEAD_DIM, 2 * ns), lambda b, r: (first + b, 0, 0, 0)),
            pl.BlockSpec((None, 2, 2, 5 * SCAN_ROWS, ns), lambda b, r: (first + b, 0, 0, 0, 0)),
            pl.BlockSpec((None, 2, 2, SCAN_ROWS, ns), lambda b, r: (b, 0, 0, 0, 0)),
        ],
        out_specs=(
            pl.BlockSpec((rb, 2 * ns), lambda b, r: (r, b)),
            pl.BlockSpec((rb, 2 * ns), lambda b, r: (nrb - 1 - r, b)),
            pl.BlockSpec((None, 2, 2, SCAN_ROWS, ns), lambda b, r: (b, 0, 0, 0, 0)),
        ),
        scratch_shapes=[pltpu.VMEM((2, rb, 2 * ns), F32), pltpu.VMEM((2, 2, SCAN_ROWS, ns), F32)],
        compiler_params=_params(("arbitrary", "arbitrary")),
        name="ssm_states",
    )(u, u, w, sc, carry_in)


def _ssm_y_kernel(u_ref, hf_ref, hb_ref, m_ref, v_ref, y_ref, *, rb):
    ycat = (_dot(_chunk_rows(u_ref, rb), m_ref[...]) + _dot(hf_ref[...], v_ref[0]) + _dot(hb_ref[...], v_ref[1]))
    for t in range(SSM_T):
        y_ref[pl.ds(t, rb, stride=SSM_T), :] = ycat[:, t * HEAD_DIM:(t + 1) * HEAD_DIM]


def _ssm_y(u, hf, hb, m, v, first):
    n_tok, width = u.shape
    nb = width // HEAD_DIM
    n_chunks = n_tok // SSM_T
    rb = min(SSM_ROW_BLOCK, n_chunks)
    ns = SSM_BLOCK_STATES
    cw = SSM_T * HEAD_DIM
    return pl.pallas_call(
        functools.partial(_ssm_y_kernel, rb=rb),
        out_shape=jax.ShapeDtypeStruct((n_tok, width), F32),
        grid=(nb, n_chunks // rb),
        in_specs=[
            pl.BlockSpec((rb * SSM_T, HEAD_DIM), lambda b, r: (r, b)),
            pl.BlockSpec((rb, 2 * ns), lambda b, r: (r, b)),
            pl.BlockSpec((rb, 2 * ns), lambda b, r: (r, b)),
            pl.BlockSpec((None, cw, cw), lambda b, r: (first + b, 0, 0)),
            pl.BlockSpec((None, 2, 2 * ns, cw), lambda b, r: (first + b, 0, 0, 0)),
        ],
        out_specs=pl.BlockSpec((rb * SSM_T, HEAD_DIM), lambda b, r: (r, b)),
        compiler_params=_params(("arbitrary", "arbitrary")),
        name="ssm_y",
    )(u, hf, hb, m, v)


def _outproj_kernel(*refs, n_attn, final):
    mix = refs[:n_attn]
    y_ref, u_ref, sg_ref, d_ref, wg_ref, bg_ref, w_ref, x_ref, gate_ref = refs[n_attn:n_attn + 9]
    rest = refs[n_attn + 9:]
    y = y_ref[...] + d_ref[...] * u_ref[...].astype(F32)
    y = jax.nn.gelu(y, approximate=True)
    z = _dot(y.astype(BF16), wg_ref[...]) + bg_ref[...]
    mix_s = (y * jax.nn.sigmoid(z) * sg_ref[...].astype(F32)).astype(BF16)
    acc = None
    start = 0
    for a in [m[...] for m in mix] + [mix_s]:
        part = _dot(a, w_ref[start:start + a.shape[1], :])
        acc = part if acc is None else acc + part
        start += a.shape[1]
    x = x_ref[...] + gate_ref[...] * acc
    if final:
        fg_ref, o_ref = rest
        x = x * lax.rsqrt(jnp.mean(x * x, axis=-1, keepdims=True) + NORM_EPS) * fg_ref[...]
    else:
        (o_ref,) = rest
    o_ref[...] = x


def _outproj(attn_mixes, y, p, d_skip, w_glu, b_glu, w_out, x, gate, final_gain, bm):
    n, d = x.shape
    s_width = y.shape[1]
    blk = s_width // HEAD_DIM
    row = lambda i: (i, 0)
    const = lambda i: (0, 0)
    in_specs = [pl.BlockSpec((bm, a.shape[1]), row) for a in attn_mixes]
    in_specs += [
        pl.BlockSpec((bm, s_width), row),
        pl.BlockSpec((bm, s_width), lambda i: (i, SLOT_SU // blk)),
        pl.BlockSpec((bm, s_width), lambda i: (i, SLOT_SG // blk)),
        pl.BlockSpec((1, s_width), const),
        pl.BlockSpec((s_width, s_width), const),
        pl.BlockSpec((1, s_width), const),
        pl.BlockSpec(w_out.shape, const, pipeline_mode=pl.Buffered(1)),
        pl.BlockSpec((bm, d), row),
        pl.BlockSpec((1, d), const),
    ]
    args = list(attn_mixes) + [y, p, p, d_skip, w_glu, b_glu, w_out, x, gate]
    final = final_gain is not None
    if final:
        in_specs.append(pl.BlockSpec((1, d), const))
        args.append(final_gain)
    return pl.pallas_call(
        functools.partial(_outproj_kernel, n_attn=len(attn_mixes), final=final),
        out_shape=jax.ShapeDtypeStruct((n, d), F32),
        grid=(n // bm,),
        in_specs=in_specs,
        out_specs=pl.BlockSpec((bm, d), row),
        compiler_params=_params(("arbitrary",)),
        name="outproj_final" if final else "outproj",
    )(*args)


def _rope_tables(n):
    pos = np.arange(n)
    half = HEAD_DIM // 4
    freqs = (ROPE_THETA ** (-np.arange(half, dtype=np.float32) / half)).astype(np.float32)
    ang_r = ((pos // GRID_W).astype(np.float32)[:, None] * freqs[None, :]).astype(np.float64)
    ang_c = ((pos % GRID_W).astype(np.float32)[:, None] * freqs[None, :]).astype(np.float64)
    cos = np.concatenate([np.cos(ang_r)] * 2 + [np.cos(ang_c)] * 2, axis=-1)
    sin = np.concatenate([-np.sin(ang_r), np.sin(ang_r), -np.sin(ang_c), np.sin(ang_c)], axis=-1)
    return jnp.asarray(cos, F32), jnp.asarray(sin, F32)


def _reorder_w_in(w):
    hd = HEAD_DIM
    sizes = (6 * hd, 2 * hd, 2 * hd, 6 * hd, 6 * hd, 2 * hd, 2 * hd, 6 * hd, 4 * hd, 4 * hd)
    names = ("aq", "ak", "av", "ag", "bq", "bk", "bv", "bg", "su", "sg")
    parts, start = {}, 0
    for name, size in zip(names, sizes):
        parts[name] = w[:, start:start + size]
        start += size
    order = ("aq", "bq", "ak", "bk", "av", "bv", "su", "ag", "bg", "sg")
    return jnp.concatenate([parts[k] for k in order], axis=1).astype(BF16)


def kernel(x, c, ctx, c_ctx, w_ada, b_ada, norm_gain, w_in, a_q_gain, a_k_gain, b_sink, ssm_lambda_re,
           ssm_lambda_im, ssm_log_step, ssm_b_re, ssm_b_im, ssm_c_re, ssm_c_im, ssm_d, w_glu, b_glu, w_out,
           final_gain):
    depth, d = norm_gain.shape
    n = x.shape[1]
    n_ctx = ctx.shape[1]
    assert x.shape[0] == 1 and n % 512 == 0 and n_ctx % (SSM_T * SCAN_ROWS) == 0
    xs = x[0]
    cs = ctx[0]
    bm = 512

    c_t = jnp.concatenate([c.reshape(d, 1), c_ctx.reshape(d, 1)], axis=1)
    mod = _modulation(c_t, w_ada, b_ada)
    tables = _rope_tables(n)

    fold = lambda a: jnp.moveaxis(a, 0, 1).reshape((a.shape[1], depth * a.shape[2]) + a.shape[3:])
    m_op, w_op, v_op, scan_c = _ssm_prep(fold(ssm_lambda_re), fold(ssm_lambda_im), fold(ssm_log_step),
                                        fold(ssm_b_re), fold(ssm_b_im), fold(ssm_c_re), fold(ssm_c_im))
    blocks_per_layer = ssm_lambda_re.shape[2] // SSM_BLOCK_GROUPS

    for layer in range(depth):
        last = layer == depth - 1
        shift, scale, gate = (mod[layer, :, i * d:(i + 1) * d] for i in range(3))
        w = _reorder_w_in(w_in[layer])
        gain = norm_gain[layer].reshape(1, d)
        qk_gain = jnp.stack([a_q_gain[layer], a_k_gain[layer]], axis=0)
        p_lat, qt_lat, vt_lat, su_lat = _inproj(xs, shift[0:1], scale[0:1], gain, w, qk_gain, tables, bm)
        p_ctx, _, vt_ctx, su_ctx = _inproj(cs, shift[1:2], scale[1:2], gain, w, qk_gain, None, n_ctx)

        mix_a = _attn_global(p_lat, p_ctx, qt_lat, vt_lat, vt_ctx, bq=_largest_block(n, 2048),
                             bk=_largest_block(n + n_ctx, 1280))
        mix_b = _attn_window(p_lat, p_ctx, b_sink[layer], bq=_largest_block(n, 1024))

        first = layer * blocks_per_layer
        no_state = jnp.zeros((blocks_per_layer, 2, 2, SCAN_ROWS, SSM_BLOCK_STATES), F32)
        hf_ctx, hb_ctx, ctx_state = _ssm_states(su_ctx, w_op, scan_c, no_state, first)
        hf_lat, hb_lat, _ = _ssm_states(su_lat, w_op, scan_c, ctx_state, first)
        y_lat = _ssm_y(su_lat, hf_lat, hb_lat, m_op, v_op, first)
        d_skip = ssm_d[layer].reshape(1, -1)
        wg = w_glu[layer].astype(BF16)
        bg = b_glu[layer].reshape(1, -1)
        wo = w_out[layer].astype(BF16)
        xs_new = _outproj((mix_a, mix_b), y_lat, p_lat, d_skip, wg, bg, wo, xs, gate[0:1],
                          final_gain.reshape(1, d) if last else None, bm)
        if not last:
            sinks = jnp.stack([jnp.full_like(b_sink[layer], NEG_INF), b_sink[layer]], axis=0)
            mix_ab_c = _attn_ctx(p_ctx, sinks)
            y_ctx = _ssm_y(su_ctx, hf_ctx, hb_ctx, m_op, v_op, first)
            cs = _outproj((mix_ab_c,), y_ctx, p_ctx, d_skip, wg, bg, wo, cs, gate[1:2], None, n_ctx)
        xs = xs_new
    return xs[None]
```
